```python
import math
import jax, jax.numpy as jnp
from jax import lax
import numpy as np

D_MODEL = 1024
BATCH = 32
SEQ = 2048
DEPTH = 1

N_HEADS = 8
N_KV_HEADS = 2
HEAD_DIM = 64
KV_REP = N_HEADS // N_KV_HEADS
Q_BLOCK = 128
ROPE_THETA = 10000.0
GRID_W = 64
ROT_HALF = HEAD_DIM // 2
SSM_WIDTH = 256
SSM_GROUP = 16
SSM_GROUPS = SSM_WIDTH // SSM_GROUP
SSM_STATE = 64
N_DIR = 2
N_EXPERTS = 64
EXPERT_DIM = 128
TOP_K = 8
N_EXPERT_GROUPS = 8
TOPK_GROUPS = 4
EXPERTS_PER_GROUP = N_EXPERTS // N_EXPERT_GROUPS
SHARED_DIM = 256
ROUTED_SCALE = 2.5
PLE_DIM = 256
EPS = 1e-6

Q_W = N_HEADS * HEAD_DIM
KV_W = N_KV_HEADS * HEAD_DIM
IN_WIDTH = Q_W + 2 * KV_W + SSM_WIDTH + 2 * D_MODEL
IN_SPLITS = (Q_W, Q_W + KV_W, Q_W + 2 * KV_W, Q_W + 2 * KV_W + SSM_WIDTH,
             Q_W + 2 * KV_W + SSM_WIDTH + D_MODEL)

kernel_name = "hybrid_gqa_s5_moe_encoder"


def rmsnorm(x, g):
    xf = x.astype(jnp.float32)
    y = xf * lax.rsqrt(jnp.mean(xf * xf, axis=-1, keepdims=True) + EPS)
    return (y * g.astype(jnp.float32)).astype(x.dtype)


def _rope_half(x, ang):
    cos = jnp.cos(ang)[None, :, None, :]
    sin = jnp.sin(ang)[None, :, None, :]
    x1, x2 = jnp.split(x, 2, axis=-1)
    return jnp.concatenate([x1 * cos - x2 * sin, x2 * cos + x1 * sin], axis=-1)


def axial_rope(x, row_ang, col_ang):
    xf = x.astype(jnp.float32)
    out = jnp.concatenate([_rope_half(xf[..., :ROT_HALF], row_ang),
                           _rope_half(xf[..., ROT_HALF:], col_ang)], axis=-1)
    return out.astype(x.dtype)


def block_attention(q, k, v):
    B, L = q.shape[0], q.shape[1]
    n_blk = L // Q_BLOCK
    scale = 1.0 / math.sqrt(HEAD_DIM)
    qb = q.reshape(B, n_blk, Q_BLOCK, N_KV_HEADS, KV_REP, HEAD_DIM).transpose(1, 0, 2, 3, 4, 5)

    def attend_block(q_blk):
        s = jnp.einsum('bqgrd,bkgd->bgrqk', q_blk, k, preferred_element_type=jnp.float32) * scale
        pr = jax.nn.softmax(s, axis=-1).astype(v.dtype)
        return jnp.einsum('bgrqk,bkgd->bqgrd', pr, v)

    ob = lax.map(attend_block, qb)
    return ob.transpose(1, 0, 2, 3, 4, 5).reshape(B, L, Q_W)


def _lin_combine(e1, e2):
    a1, b1 = e1
    a2, b2 = e2
    return a1 * a2, a2 * b1 + b2


def _s5_scan(u, a_re, a_im, log_dt, b_re, b_im, c_re, c_im):
    lam = lax.complex(a_re, a_im)
    dt = jnp.exp(log_dt)[:, None]
    a_bar = jnp.exp(lam * dt)
    b_bar = ((a_bar - 1.0) / lam)[..., None] * lax.complex(b_re, b_im)
    bu = jnp.einsum('gpc,blgc->blgp', b_bar, u.astype(jnp.complex64))
    a_seq = jnp.broadcast_to(a_bar, (1, u.shape[1]) + a_bar.shape)
    _, state = lax.associative_scan(_lin_combine, (a_seq, bu), axis=1)
    return jnp.real(jnp.einsum('gcp,blgp->blgc', lax.complex(c_re, c_im), state))


def s5_branch(u_raw, a_re, a_im, log_dt, b_re, b_im, c_re, c_im, d_skip, w_glu, b_glu):
    B, L = u_raw.shape[0], u_raw.shape[1]
    f32 = jnp.float32
    u = u_raw.astype(f32).reshape(B, L, SSM_GROUPS, SSM_GROUP)
    y = d_skip.astype(f32).reshape(SSM_GROUPS, SSM_GROUP) * u
    for d in range(N_DIR):
        args = [t[d].astype(f32) for t in (a_re, a_im, log_dt, b_re, b_im, c_re, c_im)]
        if d == 0:
            y = y + _s5_scan(u, *args)
        else:
            y = y + jnp.flip(_s5_scan(jnp.flip(u, axis=1), *args), axis=1)
    y = jax.nn.gelu(y.reshape(B, L, SSM_WIDTH))
    y = y * jax.nn.sigmoid(y @ w_glu.astype(f32) + b_glu.astype(f32))
    return y.astype(u_raw.dtype)


def moe(xn, w_router, router_bias, w_exp_gate, w_exp_up, w_exp_down, w_sh_gate, w_sh_up, w_sh_down):
    B, L, D = xn.shape
    f32 = jnp.float32
    xf = xn.reshape(B * L, D)
    scores = jax.nn.sigmoid(jnp.dot(xf, w_router, preferred_element_type=f32))
    biased = scores + router_bias.astype(f32)
    grp = biased.reshape(-1, N_EXPERT_GROUPS, EXPERTS_PER_GROUP)
    grp_score = jnp.sum(lax.top_k(grp, 2)[0], axis=-1)
    _, grp_idx = lax.top_k(grp_score, TOPK_GROUPS)
    grp_mask = jnp.sum(jax.nn.one_hot(grp_idx, N_EXPERT_GROUPS, dtype=f32), axis=1)
    exp_mask = jnp.repeat(grp_mask, EXPERTS_PER_GROUP, axis=-1) > 0.0
    masked = jnp.where(exp_mask, biased, -1e30)
    _, top_idx = lax.top_k(masked, TOP_K)
    top_w = jnp.take_along_axis(scores, top_idx, axis=-1)
    top_w = top_w / jnp.sum(top_w, axis=-1, keepdims=True) * ROUTED_SCALE
    gates = jnp.einsum('nk,nke->en', top_w,
                       jax.nn.one_hot(top_idx, N_EXPERTS, dtype=f32)).astype(xn.dtype)

    def expert_step(acc, ew):
        wg, wu, wd, g = ew
        hid = jax.nn.silu(xf @ wg) * (xf @ wu)
        return acc + (hid * g[:, None]) @ wd, None

    routed, _ = lax.scan(expert_step, jnp.zeros_like(xf), (w_exp_gate, w_exp_up, w_exp_down, gates))
    shared = (jax.nn.silu(xf @ w_sh_gate) * (xf @ w_sh_up)) @ w_sh_down
    return (routed + shared).reshape(B, L, D)


def setup_inputs(seed: int = 0) -> dict:
    key = jax.random.key(seed)
    ks = jax.random.split(key, 40)
    nrm = jax.random.normal
    f32 = jnp.float32
    P, G, E, F = SSM_STATE, SSM_GROUPS, N_EXPERTS, EXPERT_DIM
    inp = {}
    inp['x'] = nrm(ks[0], (BATCH, SEQ, D_MODEL), f32)
    inp['p'] = nrm(ks[1], (DEPTH, BATCH, SEQ, PLE_DIM), f32)
    inp['g_mix'] = 1.0 + 0.02 * nrm(ks[2], (DEPTH, D_MODEL), f32)
    inp['w_in'] = nrm(ks[3], (DEPTH, D_MODEL, IN_WIDTH), f32) * D_MODEL ** -0.5
    inp['q_norm'] = 1.0 + 0.02 * nrm(ks[4], (DEPTH, HEAD_DIM), f32)
    inp['k_norm'] = 1.0 + 0.02 * nrm(ks[5], (DEPTH, HEAD_DIM), f32)
    inp['w_attn_up'] = nrm(ks[6], (DEPTH, Q_W, D_MODEL), f32) * Q_W ** -0.5
    inp['a_re'] = -0.5 + 0.01 * nrm(ks[7], (DEPTH, N_DIR, G, P), f32)
    inp['a_im'] = math.pi * jnp.arange(P, dtype=f32) + 0.01 * nrm(ks[8], (DEPTH, N_DIR, G, P), f32)
    inp['log_dt'] = jax.random.uniform(ks[9], (DEPTH, N_DIR, G), f32, math.log(1e-3), math.log(1e-1))
    inp['b_re'] = nrm(ks[10], (DEPTH, N_DIR, G, P, SSM_GROUP), f32) * (2 * SSM_GROUP) ** -0.5
    inp['b_im'] = nrm(ks[11], (DEPTH, N_DIR, G, P, SSM_GROUP), f32) * (2 * SSM_GROUP) ** -0.5
    inp['c_re'] = nrm(ks[12], (DEPTH, N_DIR, G, SSM_GROUP, P), f32) * (2 * P) ** -0.5
    inp['c_im'] = nrm(ks[13], (DEPTH, N_DIR, G, SSM_GROUP, P), f32) * (2 * P) ** -0.5
    inp['d_skip'] = nrm(ks[14], (DEPTH, SSM_WIDTH), f32)
    inp['w_glu'] = nrm(ks[15], (DEPTH, SSM_WIDTH, SSM_WIDTH), f32) * SSM_WIDTH ** -0.5
    inp['b_glu'] = 0.01 * nrm(ks[16], (DEPTH, SSM_WIDTH), f32)
    inp['w_ssm_up'] = nrm(ks[17], (DEPTH, SSM_WIDTH, D_MODEL), f32) * SSM_WIDTH ** -0.5
    inp['w_out'] = nrm(ks[18], (DEPTH, D_MODEL, D_MODEL), f32) * D_MODEL ** -0.5
    inp['g_ffn'] = 1.0 + 0.02 * nrm(ks[19], (DEPTH, D_MODEL), f32)
    inp['w_router'] = nrm(ks[20], (DEPTH, D_MODEL, E), f32) * D_MODEL ** -0.5
    inp['router_bias'] = 0.01 * nrm(ks[21], (DEPTH, E), f32)
    inp['w_exp_gate'] = nrm(ks[22], (DEPTH, E, D_MODEL, F), f32) * D_MODEL ** -0.5
    inp['w_exp_up'] = nrm(ks[23], (DEPTH, E, D_MODEL, F), f32) * D_MODEL ** -0.5
    inp['w_exp_down'] = nrm(ks[24], (DEPTH, E, F, D_MODEL), f32) * F ** -0.5
    inp['w_sh_gate'] = nrm(ks[25], (DEPTH, D_MODEL, SHARED_DIM), f32) * D_MODEL ** -0.5
    inp['w_sh_up'] = nrm(ks[26], (DEPTH, D_MODEL, SHARED_DIM), f32) * D_MODEL ** -0.5
    inp['w_sh_down'] = nrm(ks[27], (DEPTH, SHARED_DIM, D_MODEL), f32) * SHARED_DIM ** -0.5
    inp['g_ple'] = 1.0 + 0.02 * nrm(ks[28], (DEPTH, D_MODEL), f32)
    inp['w_ple_gate'] = nrm(ks[29], (DEPTH, D_MODEL, D_MODEL), f32) * D_MODEL ** -0.5
    inp['w_ple_proj'] = nrm(ks[30], (DEPTH, PLE_DIM, D_MODEL), f32) * PLE_DIM ** -0.5
    inp['g_final'] = 1.0 + 0.02 * nrm(ks[31], (D_MODEL,), f32)
    return inp


def reference(x, p, g_mix, w_in, q_norm, k_norm, w_attn_up, a_re, a_im, log_dt, b_re, b_im,
              c_re, c_im, d_skip, w_glu, b_glu, w_ssm_up, w_out, g_ffn, w_router, router_bias,
              w_exp_gate, w_exp_up, w_exp_down, w_sh_gate, w_sh_up, w_sh_down, g_ple,
              w_ple_gate, w_ple_proj, g_final):
    B, L, _ = x.shape
    f32 = jnp.float32
    ROWS = L // GRID_W
    row_ids = jnp.repeat(jnp.arange(ROWS, dtype=f32), GRID_W)
    col_ids = jnp.tile(jnp.arange(GRID_W, dtype=f32), ROWS)
    inv_freq = ROPE_THETA ** (-jnp.arange(0, ROT_HALF, 2, dtype=f32) / ROT_HALF)
    row_ang = row_ids[:, None] * inv_freq[None, :]
    col_ang = col_ids[:, None] * inv_freq[None, :]

    h = x
    for i in range(DEPTH):
        xn = rmsnorm(h, g_mix[i])
        proj = xn @ w_in[i]
        q, k, v, u, gate_a, gate_s = jnp.split(proj, IN_SPLITS, axis=-1)
        q = axial_rope(rmsnorm(q.reshape(B, L, N_HEADS, HEAD_DIM), q_norm[i]), row_ang, col_ang)
        k = axial_rope(rmsnorm(k.reshape(B, L, N_KV_HEADS, HEAD_DIM), k_norm[i]), row_ang, col_ang)
        v = v.reshape(B, L, N_KV_HEADS, HEAD_DIM)
        attn = block_attention(q, k, v)
        ssm = s5_branch(u, a_re[i], a_im[i], log_dt[i], b_re[i], b_im[i], c_re[i], c_im[i],
                        d_skip[i], w_glu[i], b_glu[i])
        merged = (jax.nn.sigmoid(gate_a) * (attn @ w_attn_up[i])
                  + jax.nn.sigmoid(gate_s) * (ssm @ w_ssm_up[i]))
        h = h + merged @ w_out[i]
        h = h + moe(rmsnorm(h, g_ffn[i]), w_router[i], router_bias[i], w_exp_gate[i], w_exp_up[i],
                    w_exp_down[i], w_sh_gate[i], w_sh_up[i], w_sh_down[i])
        ple_gate = jax.nn.sigmoid(rmsnorm(h, g_ple[i]) @ w_ple_gate[i])
        h = h + ple_gate * (p[i] @ w_ple_proj[i])
    return rmsnorm(h, g_final)
```

```python
import functools
import math

import jax
import jax.numpy as jnp
from jax import lax
from jax.experimental import pallas as pl
from jax.experimental.pallas import tpu as pltpu

F32 = jnp.float32
BF16 = jnp.bfloat16

N_HEADS = 8
N_KV_HEADS = 2
HEAD_DIM = 64
ROPE_THETA = 10000.0
GRID_W = 64
ROT_HALF = HEAD_DIM // 2
ROT_QUARTER = ROT_HALF // 2
SSM_WIDTH = 256
SSM_GROUP = 16
SSM_GROUPS = SSM_WIDTH // SSM_GROUP
SSM_STATE = 64
SSM_LANES = SSM_GROUPS * SSM_STATE
N_EXPERTS = 64
EXPERT_DIM = 128
TOP_K = 8
N_EXPERT_GROUPS = 8
TOPK_GROUPS = 4
EXPERTS_PER_GROUP = N_EXPERTS // N_EXPERT_GROUPS
ROUTED_SCALE = 2.5
EPS = 1e-6
Q_W = N_HEADS * HEAD_DIM
KV_W = N_KV_HEADS * HEAD_DIM

LANES = 128
SUBLANES = 8
VMEM_LIMIT = 56 * 1024 * 1024

PROJ_TM = 512
ATTN_TQ = 256
SSM_T = 128
SSM_B = SUBLANES
SSM_S = SSM_T + SUBLANES
MERGE_TM = 512
MOE_TM = 1024
MOE_EC = 8
PLE_TM = 512


def _cparams(*sem):
    return pltpu.CompilerParams(dimension_semantics=sem, vmem_limit_bytes=VMEM_LIMIT)


def _const_spec(shape):
    nd = len(shape)
    return pl.BlockSpec(shape, lambda *_: (0,) * nd)


def _dot(a, b):
    return jnp.dot(a, b, preferred_element_type=F32)


def _dot_nt(a, b):
    return lax.dot_general(a, b, (((1,), (1,)), ((), ())), preferred_element_type=F32)


def _split_bf16(x):
    hi = x.astype(BF16)
    lo = (x - hi.astype(F32)).astype(BF16)
    return hi, lo


def _rms(x, g):
    ms = jnp.mean(x * x, axis=-1, keepdims=True)
    return x * lax.rsqrt(ms + EPS) * g


def _sigmoid(x):
    return 1.0 / (1.0 + jnp.exp(-x))


def _silu(x):
    return x * _sigmoid(x)


def _gelu_tanh(x):
    c = math.sqrt(2.0 / math.pi)
    return 0.5 * x * (1.0 + jnp.tanh(c * (x + 0.044715 * (x * x * x))))


def _ssm_prep_kernel(are, aim, ldt, bre, bim, oar, oai, obr, obi):
    dt = jnp.exp(ldt[...])
    lr = are[...]
    li = aim[...]
    mag = jnp.exp(lr * dt)
    ar = mag * jnp.cos(li * dt)
    ai = mag * jnp.sin(li * dt)
    xr = ar - 1.0
    den = lr * lr + li * li
    qr = (xr * lr + ai * li) / den
    qi = (ai * lr - xr * li) / den
    oar[...] = ar
    oai[...] = ai
    obr[...] = qr * bre[...] - qi * bim[...]
    obi[...] = qr * bim[...] + qi * bre[...]


def _ssm_prep(a_re, a_im, log_dt, b_re, b_im):
    n_dir = a_re.shape[0]
    rows = n_dir * SSM_GROUPS * SSM_STATE
    full = (n_dir, SSM_GROUPS, SSM_STATE, SSM_GROUP)
    bc = lambda t: jnp.broadcast_to(t, full).reshape(rows, SSM_GROUP).astype(F32)
    args = (bc(a_re[..., None]), bc(a_im[..., None]), bc(log_dt[..., None, None]),
            b_re.reshape(rows, SSM_GROUP).astype(F32), b_im.reshape(rows, SSM_GROUP).astype(F32))
    shp = jax.ShapeDtypeStruct((rows, SSM_GROUP), F32)
    oar, oai, obr, obi = pl.pallas_call(
        _ssm_prep_kernel, out_shape=(shp, shp, shp, shp), name="ssm_prep")(*args)
    rs = lambda t: t.reshape(full)
    return rs(oar)[..., 0], rs(oai)[..., 0], rs(obr), rs(obi)


def _proj_kernel(x_ref, g_ref, w_ref, qg_ref, kg_ref, cos_ref, sin_ref, ones_ref,
                 q_ref, k_ref, v_ref, u_ref, ga_ref, gs_ref):
    tm = x_ref.shape[0]
    xn = _rms(x_ref[...], g_ref[...]).astype(BF16)
    proj = _dot(xn, w_ref[...])
    cos = cos_ref[...]
    sin = sin_ref[...]
    ones = ones_ref[...]
    lane = lax.broadcasted_iota(jnp.int32, (tm, LANES), 1)
    first = (lane % ROT_HALF) < ROT_QUARTER

    def norm_rope(t, gain):
        hi, lo = _split_bf16(t * t)
        ss = _dot(hi, ones) + _dot(lo, ones)
        tn = t * lax.rsqrt(ss * (1.0 / HEAD_DIM) + EPS) * gain
        partner = jnp.where(first, pltpu.roll(tn, LANES - ROT_QUARTER, 1),
                            pltpu.roll(tn, ROT_QUARTER, 1))
        return tn * cos + partner * sin

    scale = 1.0 / math.sqrt(HEAD_DIM)
    for j in range(Q_W // LANES):
        sl = slice(j * LANES, (j + 1) * LANES)
        q_ref[:, sl] = (norm_rope(proj[:, sl], qg_ref[...]) * scale).astype(BF16)
    o = Q_W
    k_ref[...] = norm_rope(proj[:, o:o + KV_W], kg_ref[...]).astype(BF16)
    o += KV_W
    v_ref[...] = proj[:, o:o + KV_W].astype(BF16)
    o += KV_W
    u_ref[...] = proj[:, o:o + SSM_WIDTH].astype(BF16)
    o += SSM_WIDTH
    d = ga_ref.shape[1]
    ga_ref[...] = _sigmoid(proj[:, o:o + d]).astype(BF16)
    o += d
    gs_ref[...] = _sigmoid(proj[:, o:o + d]).astype(BF16)


def _proj(x2, g_mix, w_cat, qg, kg, cos_t, sin_t, ones_bd, seq_len):
    n, d = x2.shape
    tm = PROJ_TM
    per_seq = seq_len // tm
    row = lambda w: pl.BlockSpec((tm, w), lambda i: (i, 0))
    pos = pl.BlockSpec((tm, LANES), lambda i: (i % per_seq, 0))
    out = lambda w: jax.ShapeDtypeStruct((n, w), BF16)
    return pl.pallas_call(
        _proj_kernel,
        grid=(n // tm,),
        in_specs=[row(d), _const_spec(g_mix.shape), _const_spec(w_cat.shape),
                  _const_spec(qg.shape), _const_spec(kg.shape), pos, pos,
                  _const_spec(ones_bd.shape)],
        out_specs=[row(Q_W), row(KV_W), row(KV_W), row(SSM_WIDTH), row(d), row(d)],
        out_shape=[out(Q_W), out(KV_W), out(KV_W), out(SSM_WIDTH), out(d), out(d)],
        compiler_params=_cparams("parallel"),
        name="proj",
    )(x2, g_mix, w_cat, qg, kg, cos_t, sin_t, ones_bd)


def _attn_kernel(q_ref, k_ref, v_ref, o_ref):
    tq = q_ref.shape[1]
    k = k_ref[0]
    v = v_ref[0]
    lane = lax.broadcasted_iota(jnp.int32, (tq, LANES), 1)
    group0 = lane < HEAD_DIM
    zero = jnp.zeros((tq, LANES), BF16)
    for j in range(Q_W // LANES):
        sl = slice(j * LANES, (j + 1) * LANES)
        qc = q_ref[0, :, sl]
        outs = []
        for g in range(N_KV_HEADS):
            qm = jnp.where(group0 if g == 0 else jnp.logical_not(group0), qc, zero)
            s = _dot_nt(qm, k)
            p = jnp.exp(s - jnp.max(s, axis=-1, keepdims=True))
            denom = jnp.sum(p, axis=-1, keepdims=True)
            outs.append(_dot(p.astype(BF16), v) / denom)
        o_ref[0, :, sl] = jnp.where(group0, outs[0], outs[1]).astype(BF16)


def _attention(q, k, v):
    b, l, _ = q.shape
    tq = ATTN_TQ
    return pl.pallas_call(
        _attn_kernel,
        grid=(b, l // tq),
        in_specs=[pl.BlockSpec((1, tq, Q_W), lambda i, j: (i, j, 0)),
                  pl.BlockSpec((1, l, KV_W), lambda i, j: (i, 0, 0)),
                  pl.BlockSpec((1, l, KV_W), lambda i, j: (i, 0, 0))],
        out_specs=pl.BlockSpec((1, tq, Q_W), lambda i, j: (i, j, 0)),
        out_shape=jax.ShapeDtypeStruct((b, l, Q_W), BF16),
        compiler_params=_cparams("parallel", "parallel"),
        name="attention",
    )(q, k, v)


def _ssm_kernel(uf_ref, ub_ref, bmat_ref, cmat_ref, a_ref, yf_ref, yb_ref, buf, carry):
    t_len = uf_ref.shape[1]
    c = pl.program_id(1)

    @pl.when(c == 0)
    def _():
        carry[...] = jnp.zeros_like(carry)

    nt = SSM_LANES // LANES
    for s in range(SSM_B):
        rows = slice(s * SSM_S, s * SSM_S + t_len)
        for d, u_ref in enumerate((uf_ref, ub_ref)):
            bu = _dot(u_ref[s], bmat_ref[d])
            for c in range(2 * nt):
                buf[d * 2 * nt + c, rows, :] = bu[:, c * LANES:(c + 1) * LANES]

    def step(i, st):
        new = []
        for d in range(2):
            t = i if d == 0 else t_len - 1 - i
            rows = pl.ds(t, SSM_B, stride=SSM_S)
            for c in range(nt):
                pr, pi = d * 2 * nt + c, d * 2 * nt + nt + c
                lanes = slice(c * LANES, (c + 1) * LANES)
                ar, ai = a_ref[2 * d, :, lanes], a_ref[2 * d + 1, :, lanes]
                sr, si = st[(d * nt + c) * 2], st[(d * nt + c) * 2 + 1]
                nr = ar * sr - ai * si + buf[pr, rows, :]
                ni = ar * si + ai * sr + buf[pi, rows, :]
                buf[pr, rows, :] = nr
                buf[pi, rows, :] = ni
                new += [nr, ni]
        return tuple(new)

    init = tuple(carry[n] for n in range(4 * nt))
    st = lax.fori_loop(0, t_len, step, init)
    for n in range(4 * nt):
        carry[n] = st[n]

    for s in range(SSM_B):
        rows = slice(s * SSM_S, s * SSM_S + t_len)
        for d, y_ref in enumerate((yf_ref, yb_ref)):
            xs = jnp.concatenate([buf[d * 2 * nt + c, rows, :] for c in range(2 * nt)],
                                 axis=1).astype(BF16)
            y_ref[s] = _dot(xs, cmat_ref[d]).astype(y_ref.dtype)


def _ssm_scan(u, bmat, cmat, a_vec):
    b, l, w = u.shape
    t = SSM_T
    nc = l // t
    blk = (SSM_B, t, w)
    planes = 4 * SSM_LANES // LANES
    out = jax.ShapeDtypeStruct((b, l, w), BF16)
    return pl.pallas_call(
        _ssm_kernel,
        grid=(b // SSM_B, nc),
        in_specs=[pl.BlockSpec(blk, lambda i, c: (i, c, 0)),
                  pl.BlockSpec(blk, lambda i, c: (i, nc - 1 - c, 0)),
                  _const_spec(bmat.shape), _const_spec(cmat.shape), _const_spec(a_vec.shape)],
        out_specs=[pl.BlockSpec(blk, lambda i, c: (i, c, 0)),
                   pl.BlockSpec(blk, lambda i, c: (i, nc - 1 - c, 0))],
        out_shape=[out, out],
        scratch_shapes=[pltpu.VMEM((planes, SSM_B * SSM_S, LANES), F32),
                        pltpu.VMEM((planes, SSM_B, LANES), F32)],
        compiler_params=_cparams("parallel", "arbitrary"),
        name="ssm_scan",
    )(u, u, bmat, cmat, a_vec)


def _first_index_of_max(vals, idx, n, axis):
    m = jnp.max(vals, axis=axis, keepdims=True)
    return jnp.min(jnp.where(vals == m, idx, n), axis=axis, keepdims=True)


def _route(scores, bias):
    tm = scores.shape[1]
    neg = -jnp.inf
    biased = scores + bias
    b3 = biased.reshape(N_EXPERT_GROUPS, EXPERTS_PER_GROUP, tm)
    j3 = lax.broadcasted_iota(jnp.int32, b3.shape, 1)
    m1 = jnp.max(b3, axis=1, keepdims=True)
    f1 = jnp.min(jnp.where(b3 == m1, j3, EXPERTS_PER_GROUP), axis=1, keepdims=True)
    m2 = jnp.max(jnp.where(j3 == f1, neg, b3), axis=1, keepdims=True)
    gscore = (m1 + m2)[:, 0, :]
    gi = lax.broadcasted_iota(jnp.int32, gscore.shape, 0)
    gsel = jnp.zeros(gscore.shape, F32)
    cur = gscore
    for _ in range(TOPK_GROUPS):
        pick = gi == _first_index_of_max(cur, gi, N_EXPERT_GROUPS, 0)
        gsel = jnp.where(pick, 1.0, gsel)
        cur = jnp.where(pick, neg, cur)
    emask = jnp.broadcast_to(gsel[:, None, :], b3.shape) > 0.0
    masked = jnp.where(emask, b3, -1e30).reshape(N_EXPERTS, tm)
    ei = lax.broadcasted_iota(jnp.int32, masked.shape, 0)
    top_w = jnp.zeros(masked.shape, F32)
    cur = masked
    for _ in range(TOP_K):
        pick = ei == _first_index_of_max(cur, ei, N_EXPERTS, 0)
        top_w = jnp.where(pick, scores, top_w)
        cur = jnp.where(pick, neg, cur)
    return top_w / jnp.sum(top_w, axis=0, keepdims=True) * ROUTED_SCALE


def _merge_kernel(attn_ref, yf_ref, yb_ref, u_ref, ga_ref, gs_ref, x_ref,
                  dskip_ref, wglu_ref, bglu_ref, wau_ref, wsu_ref, wout_ref, gffn_ref,
                  wrh_ref, wrl_ref, rbias_ref,
                  h_ref, xn_ref, gates_ref):
    y = (dskip_ref[...] * u_ref[...].astype(F32) + yf_ref[...].astype(F32)
         + yb_ref[...].astype(F32))
    y = _gelu_tanh(y)
    ssm = y * _sigmoid(_dot(y.astype(BF16), wglu_ref[...]) + bglu_ref[...])
    merged = (ga_ref[...].astype(F32) * _dot(attn_ref[...], wau_ref[...])
              + gs_ref[...].astype(F32) * _dot(ssm.astype(BF16), wsu_ref[...]))
    h = x_ref[...] + _dot(merged.astype(BF16), wout_ref[...])
    h_ref[...] = h
    xn = _rms(h, gffn_ref[...])
    xn_ref[...] = xn.astype(BF16)
    xh, xl = _split_bf16(xn)
    wh = wrh_ref[...]
    logits = _dot_nt(wh, xh) + _dot_nt(wh, xl) + _dot_nt(wrl_ref[...], xh)
    gates_ref[...] = _route(_sigmoid(logits), rbias_ref[...]).T


def _merge(attn, yf, yb, u, ga, gs, x2, dskip, wglu, bglu, wau, wsu, wout, gffn, wrh, wrl, rbias):
    n, d = x2.shape
    tm = MERGE_TM
    row = lambda w: pl.BlockSpec((tm, w), lambda i: (i, 0))
    consts = (dskip, wglu, bglu, wau, wsu, wout, gffn, wrh, wrl, rbias)
    return pl.pallas_call(
        _merge_kernel,
        grid=(n // tm,),
        in_specs=[row(Q_W), row(SSM_WIDTH), row(SSM_WIDTH), row(SSM_WIDTH), row(d), row(d), row(d)]
                 + [_const_spec(c.shape) for c in consts],
        out_specs=[row(d), row(d), row(N_EXPERTS)],
        out_shape=[jax.ShapeDtypeStruct((n, d), F32), jax.ShapeDtypeStruct((n, d), BF16),
                   jax.ShapeDtypeStruct((n, N_EXPERTS), F32)],
        compiler_params=_cparams("parallel"),
        name="merge",
    )(attn, yf, yb, u, ga, gs, x2, *consts)


def _moe_kernel(x_ref, gates_ref, h_ref, wg_ref, wu_ref, wd_ref, exp_ref,
                wsg_ref, wsu_ref, wsd_ref, o_ref, acc_ref):
    j = pl.program_id(1)
    x = x_ref[...]

    @pl.when(j == 0)
    def _():
        sh = _silu(_dot(x, wsg_ref[...])) * _dot(x, wsu_ref[...])
        acc_ref[...] = h_ref[...] + _dot(sh.astype(BF16), wsd_ref[...])

    gh, gl = _split_bf16(gates_ref[...])
    ex = exp_ref[...]
    gx = _dot(gh, ex) + _dot(gl, ex)
    hid = _silu(_dot(x, wg_ref[...])) * _dot(x, wu_ref[...]) * gx
    acc_ref[...] += _dot(hid.astype(BF16), wd_ref[...])

    @pl.when(j == pl.num_programs(1) - 1)
    def _():
        o_ref[...] = acc_ref[...]


def _moe(xn, gates, h, wg, wu, wd, expand, wsg, wsu, wsd):
    n, d = xn.shape
    tm = MOE_TM
    ew = MOE_EC * EXPERT_DIM
    row = lambda w: pl.BlockSpec((tm, w), lambda i, j: (i, 0))
    return pl.pallas_call(
        _moe_kernel,
        grid=(n // tm, N_EXPERTS // MOE_EC),
        in_specs=[row(d), row(N_EXPERTS), row(d),
                  pl.BlockSpec((d, ew), lambda i, j: (0, j)),
                  pl.BlockSpec((d, ew), lambda i, j: (0, j)),
                  pl.BlockSpec((ew, d), lambda i, j: (j, 0)),
                  pl.BlockSpec((N_EXPERTS, ew), lambda i, j: (0, j)),
                  _const_spec(wsg.shape), _const_spec(wsu.shape), _const_spec(wsd.shape)],
        out_specs=row(d),
        out_shape=jax.ShapeDtypeStruct((n, d), F32),
        scratch_shapes=[pltpu.VMEM((tm, d), F32)],
        compiler_params=_cparams("parallel", "arbitrary"),
        name="moe",
    )(xn, gates, h, wg, wu, wd, expand, wsg, wsu, wsd)


def _ple_kernel(h_ref, p_ref, gple_ref, wpg_ref, wpp_ref, gfin_ref, o_ref, *, final):
    h = h_ref[...]
    gate = _sigmoid(_dot(_rms(h, gple_ref[...]).astype(BF16), wpg_ref[...]))
    h = h + gate * _dot(p_ref[...].astype(BF16), wpp_ref[...])
    o_ref[...] = _rms(h, gfin_ref[...]) if final else h


def _ple(h, p2, gple, wpg, wpp, gfin, final):
    n, d = h.shape
    tm = PLE_TM
    row = lambda w: pl.BlockSpec((tm, w), lambda i: (i, 0))
    consts = (gple, wpg, wpp, gfin)
    return pl.pallas_call(
        functools.partial(_ple_kernel, final=final),
        grid=(n // tm,),
        in_specs=[row(d), row(p2.shape[1])] + [_const_spec(c.shape) for c in consts],
        out_specs=row(d),
        out_shape=jax.ShapeDtypeStruct((n, d), F32),
        compiler_params=_cparams("parallel"),
        name="ple",
    )(h, p2, *consts)


def _rope_tables(seq_len):
    rows = seq_len // GRID_W
    row_ids = jnp.repeat(jnp.arange(rows, dtype=F32), GRID_W)
    col_ids = jnp.tile(jnp.arange(GRID_W, dtype=F32), rows)
    inv_freq = ROPE_THETA ** (-jnp.arange(0, ROT_HALF, 2, dtype=F32) / ROT_HALF)
    ra = row_ids[:, None] * inv_freq[None, :]
    ca = col_ids[:, None] * inv_freq[None, :]
    cos = jnp.concatenate([jnp.cos(ra), jnp.cos(ra), jnp.cos(ca), jnp.cos(ca)], axis=1)
    sin = jnp.concatenate([-jnp.sin(ra), jnp.sin(ra), -jnp.sin(ca), jnp.sin(ca)], axis=1)
    reps = LANES // HEAD_DIM
    return jnp.tile(cos, (1, reps)), jnp.tile(sin, (1, reps))


def _block_diag(t, eye):
    n_dir, g, a, b = t.shape
    return jnp.einsum('dgab,gh->dgahb', t, eye).reshape(n_dir, g * a, g * b)


def _layer(h2, p2, seq_len, final, prm, cos_t, sin_t):
    (g_mix, w_in, q_norm, k_norm, w_attn_up, a_re, a_im, log_dt, b_re, b_im, c_re, c_im,
     d_skip, w_glu, b_glu, w_ssm_up, w_out, g_ffn, w_router, router_bias, w_exp_gate,
     w_exp_up, w_exp_down, w_sh_gate, w_sh_up, w_sh_down, g_ple, w_ple_gate, w_ple_proj,
     g_final) = prm
    n, d = h2.shape
    bsz = n // seq_len
    row = lambda t: t.reshape(1, -1).astype(F32)

    rep = N_HEADS // N_KV_HEADS
    order = [g * rep + j for j in range(rep) for g in range(N_KV_HEADS)]
    qcols = jnp.concatenate([jnp.arange(HEAD_DIM) + hd * HEAD_DIM for hd in order])
    w_cat = jnp.concatenate([w_in[:, :Q_W][:, qcols], w_in[:, Q_W:]], axis=1).astype(BF16)
    reps = LANES // HEAD_DIM
    qg = jnp.tile(row(q_norm), (1, reps))
    kg = jnp.tile(row(k_norm), (1, reps))
    li = jnp.arange(LANES)
    ones_bd = (li[:, None] // HEAD_DIM == li[None, :] // HEAD_DIM).astype(BF16)

    q, k, v, u, ga, gs = _proj(h2, row(g_mix), w_cat, qg, kg, cos_t, sin_t, ones_bd, seq_len)

    attn = _attention(q.reshape(bsz, seq_len, Q_W), k.reshape(bsz, seq_len, KV_W),
                      v.reshape(bsz, seq_len, KV_W)).reshape(n, Q_W)

    abar_re, abar_im, bbar_re, bbar_im = _ssm_prep(a_re, a_im, log_dt, b_re, b_im)
    eye = jnp.eye(SSM_GROUPS, dtype=F32)
    tr = lambda t: jnp.swapaxes(t, 2, 3)
    bmat = jnp.concatenate([_block_diag(tr(bbar_re), eye), _block_diag(tr(bbar_im), eye)],
                           axis=2).astype(BF16)
    cmat = jnp.concatenate([_block_diag(tr(c_re.astype(F32)), eye),
                            _block_diag(tr(-c_im.astype(F32)), eye)], axis=1).astype(BF16)
    a_vec = jnp.stack([abar_re[0], abar_im[0], abar_re[1], abar_im[1]]).reshape(4, 1, SSM_LANES)
    a_vec = jnp.broadcast_to(a_vec, (4, SSM_B, SSM_LANES))
    yf, yb = _ssm_scan(u.reshape(bsz, seq_len, SSM_WIDTH), bmat, cmat, a_vec)

    wr_t = w_router.T.astype(F32)
    wrh, wrl = _split_bf16(wr_t)
    h1, xn, gates = _merge(
        attn, yf.reshape(n, SSM_WIDTH), yb.reshape(n, SSM_WIDTH), u, ga, gs, h2,
        row(d_skip), w_glu.astype(BF16), row(b_glu), w_attn_up[qcols, :].astype(BF16),
        w_ssm_up.astype(BF16), w_out.astype(BF16), row(g_ffn), wrh, wrl,
        router_bias.reshape(-1, 1).astype(F32))

    cat = lambda w: jnp.swapaxes(w, 0, 1).reshape(d, N_EXPERTS * EXPERT_DIM).astype(BF16)
    fi = jnp.arange(N_EXPERTS * EXPERT_DIM)
    expand = (fi[None, :] // EXPERT_DIM == jnp.arange(N_EXPERTS)[:, None]).astype(BF16)
    h2n = _moe(xn, gates, h1, cat(w_exp_gate), cat(w_exp_up),
               w_exp_down.reshape(N_EXPERTS * EXPERT_DIM, d).astype(BF16), expand,
               w_sh_gate.astype(BF16), w_sh_up.astype(BF16), w_sh_down.astype(BF16))

    return _ple(h2n, p2, row(g_ple), w_ple_gate.astype(BF16), w_ple_proj.astype(BF16),
                row(g_final), final)


def kernel(x, p, g_mix, w_in, q_norm, k_norm, w_attn_up, a_re, a_im, log_dt, b_re, b_im, c_re,
           c_im, d_skip, w_glu, b_glu, w_ssm_up, w_out, g_ffn, w_router, router_bias,
           w_exp_gate, w_exp_up, w_exp_down, w_sh_gate, w_sh_up, w_sh_down, g_ple,
           w_ple_gate, w_ple_proj, g_final):
    bsz, seq_len, d = x.shape
    depth = p.shape[0]
    stacked = (g_mix, w_in, q_norm, k_norm, w_attn_up, a_re, a_im, log_dt, b_re, b_im, c_re,
               c_im, d_skip, w_glu, b_glu, w_ssm_up, w_out, g_ffn, w_router, router_bias,
               w_exp_gate, w_exp_up, w_exp_down, w_sh_gate, w_sh_up, w_sh_down, g_ple,
               w_ple_gate, w_ple_proj)
    cos_t, sin_t = _rope_tables(seq_len)
    h = x.reshape(bsz * seq_len, d)
    for i in range(depth):
        prm = tuple(t[i] for t in stacked) + (g_final,)
        h = _layer(h, p[i].reshape(bsz * seq_len, -1), seq_len, i == depth - 1, prm,
                   cos_t, sin_t)
    return h.reshape(bsz, seq_len, d)
```

```python
import functools
import math

import jax
import jax.numpy as jnp
from jax import lax
from jax.experimental import pallas as pl
from jax.experimental.pallas import tpu as pltpu

F32 = jnp.float32
BF16 = jnp.bfloat16

N_HEADS = 8
N_KV_HEADS = 2
HEAD_DIM = 64
ROPE_THETA = 10000.0
GRID_W = 64
ROT_HALF = HEAD_DIM // 2
ROT_QUARTER = ROT_HALF // 2
SSM_WIDTH = 256
SSM_GROUP = 16
SSM_GROUPS = SSM_WIDTH // SSM_GROUP
SSM_STATE = 64
SSM_LANES = SSM_GROUPS * SSM_STATE
N_EXPERTS = 64
EXPERT_DIM = 128
TOP_K = 8
N_EXPERT_GROUPS = 8
TOPK_GROUPS = 4
EXPERTS_PER_GROUP = N_EXPERTS // N_EXPERT_GROUPS
ROUTED_SCALE = 2.5
EPS = 1e-6
Q_W = N_HEADS * HEAD_DIM
KV_W = N_KV_HEADS * HEAD_DIM

LANES = 128
SUBLANES = 8
VMEM_LIMIT = 56 * 1024 * 1024

PROJ_TM = 512
ATTN_TQ = 256
SSM_T = 128
SSM_B = SUBLANES
SSM_S = SSM_T + SUBLANES
MERGE_TM = 512
MOE_TM = 1024
MOE_EC = 8
MOE_NB = 256
PLE_TM = 512


def _cparams(*sem):
    return pltpu.CompilerParams(dimension_semantics=sem, vmem_limit_bytes=VMEM_LIMIT)


def _const_spec(shape):
    nd = len(shape)
    return pl.BlockSpec(shape, lambda *_: (0,) * nd)


def _dot(a, b):
    return jnp.dot(a, b, preferred_element_type=F32)


def _dot_nt(a, b):
    return lax.dot_general(a, b, (((1,), (1,)), ((), ())), preferred_element_type=F32)


def _split_bf16(x):
    hi = x.astype(BF16)
    lo = (x - hi.astype(F32)).astype(BF16)
    return hi, lo


def _rms(x, g):
    ms = jnp.mean(x * x, axis=-1, keepdims=True)
    return x * lax.rsqrt(ms + EPS) * g


def _sigmoid(x):
    return 1.0 / (1.0 + jnp.exp(-x))


def _silu(x):
    return x * _sigmoid(x)


def _gelu_tanh(x):
    c = math.sqrt(2.0 / math.pi)
    return 0.5 * x * (1.0 + jnp.tanh(c * (x + 0.044715 * (x * x * x))))


def _ssm_prep_kernel(are, aim, ldt, bre, bim, oar, oai, obr, obi):
    dt = jnp.exp(ldt[...])
    lr = are[...]
    li = aim[...]
    mag = jnp.exp(lr * dt)
    ar = mag * jnp.cos(li * dt)
    ai = mag * jnp.sin(li * dt)
    xr = ar - 1.0
    den = lr * lr + li * li
    qr = (xr * lr + ai * li) / den
    qi = (ai * lr - xr * li) / den
    oar[...] = ar
    oai[...] = ai
    obr[...] = qr * bre[...] - qi * bim[...]
    obi[...] = qr * bim[...] + qi * bre[...]


def _ssm_prep(a_re, a_im, log_dt, b_re, b_im):
    n_dir = a_re.shape[0]
    rows = n_dir * SSM_GROUPS * SSM_STATE
    full = (n_dir, SSM_GROUPS, SSM_STATE, SSM_GROUP)
    bc = lambda t: jnp.broadcast_to(t, full).reshape(rows, SSM_GROUP).astype(F32)
    args = (bc(a_re[..., None]), bc(a_im[..., None]), bc(log_dt[..., None, None]),
            b_re.reshape(rows, SSM_GROUP).astype(F32), b_im.reshape(rows, SSM_GROUP).astype(F32))
    shp = jax.ShapeDtypeStruct((rows, SSM_GROUP), F32)
    oar, oai, obr, obi = pl.pallas_call(
        _ssm_prep_kernel, out_shape=(shp, shp, shp, shp), name="ssm_prep")(*args)
    rs = lambda t: t.reshape(full)
    return rs(oar)[..., 0], rs(oai)[..., 0], rs(obr), rs(obi)


def _proj_kernel(x_ref, g_ref, w_ref, qg_ref, kg_ref, cos_ref, sin_ref, ones_ref,
                 q_ref, k_ref, vt_ref, u_ref, ga_ref, gs_ref):
    tm = x_ref.shape[0]
    xn = _rms(x_ref[...], g_ref[...]).astype(BF16)
    proj = _dot(xn, w_ref[...])
    cos = cos_ref[...]
    sin = sin_ref[...]
    ones = ones_ref[...]
    lane = lax.broadcasted_iota(jnp.int32, (tm, LANES), 1)
    first = (lane % ROT_HALF) < ROT_QUARTER

    def norm_rope(t, gain):
        hi, lo = _split_bf16(t * t)
        ss = _dot(hi, ones) + _dot(lo, ones)
        tn = t * lax.rsqrt(ss * (1.0 / HEAD_DIM) + EPS) * gain
        partner = jnp.where(first, pltpu.roll(tn, LANES - ROT_QUARTER, 1),
                            pltpu.roll(tn, ROT_QUARTER, 1))
        return tn * cos + partner * sin

    scale = math.log2(math.e) / math.sqrt(HEAD_DIM)
    for j in range(Q_W // LANES):
        sl = slice(j * LANES, (j + 1) * LANES)
        q_ref[:, sl] = (norm_rope(proj[:, sl], qg_ref[...]) * scale).astype(BF16)
    o = Q_W
    k_ref[...] = norm_rope(proj[:, o:o + KV_W], kg_ref[...]).astype(BF16)
    o += KV_W
    vt_ref[0] = proj[:, o:o + KV_W].T.astype(BF16)
    o += KV_W
    u_ref[...] = proj[:, o:o + SSM_WIDTH].astype(BF16)
    o += SSM_WIDTH
    d = ga_ref.shape[1]
    ga_ref[...] = _sigmoid(proj[:, o:o + d]).astype(BF16)
    o += d
    gs_ref[...] = _sigmoid(proj[:, o:o + d]).astype(BF16)


def _proj(x2, g_mix, w_cat, qg, kg, cos_t, sin_t, ones_bd, seq_len):
    n, d = x2.shape
    tm = PROJ_TM
    per_seq = seq_len // tm
    row = lambda w: pl.BlockSpec((tm, w), lambda i: (i, 0))
    pos = pl.BlockSpec((tm, LANES), lambda i: (i % per_seq, 0))
    out = lambda w: jax.ShapeDtypeStruct((n, w), BF16)
    vt_spec = pl.BlockSpec((1, KV_W, tm), lambda i: (i // per_seq, 0, i % per_seq))
    vt_shape = jax.ShapeDtypeStruct((n // seq_len, KV_W, seq_len), BF16)
    return pl.pallas_call(
        _proj_kernel,
        grid=(n // tm,),
        in_specs=[row(d), _const_spec(g_mix.shape), _const_spec(w_cat.shape),
                  _const_spec(qg.shape), _const_spec(kg.shape), pos, pos,
                  _const_spec(ones_bd.shape)],
        out_specs=[row(Q_W), row(KV_W), vt_spec, row(SSM_WIDTH), row(d), row(d)],
        out_shape=[out(Q_W), out(KV_W), vt_shape, out(SSM_WIDTH), out(d), out(d)],
        compiler_params=_cparams("parallel"),
        name="proj",
    )(x2, g_mix, w_cat, qg, kg, cos_t, sin_t, ones_bd)


ONES_ROWS = 16


def _attn_kernel(q_ref, k_ref, vt_ref, o_ref):
    tq = q_ref.shape[1]
    k = k_ref[0]
    vt = vt_ref[0]
    vt_ext = jnp.concatenate([vt, jnp.ones((ONES_ROWS, vt.shape[1]), BF16)], axis=0)
    lane = lax.broadcasted_iota(jnp.int32, (tq, LANES), 1)
    group0 = lane < HEAD_DIM
    row0 = lax.broadcasted_iota(jnp.int32, (KV_W, tq), 0) < HEAD_DIM
    zero = jnp.zeros((tq, LANES), BF16)

    def scores(h):
        j, g = divmod(h, N_KV_HEADS)
        qc = q_ref[0, :, j * LANES:(j + 1) * LANES]
        qm = jnp.where(group0 if g == 0 else jnp.logical_not(group0), qc, zero)
        return _dot_nt(k, qm)

    s_next = scores(0)
    outs = []
    for h in range(N_HEADS):
        s = s_next
        if h + 1 < N_HEADS:
            s_next = scores(h + 1)
        p = jnp.exp2(s - jnp.max(s, axis=0, keepdims=True)).astype(BF16)
        r = _dot(vt_ext, p)
        outs.append(r[:KV_W] / r[KV_W:KV_W + 1])
        if h % N_KV_HEADS == N_KV_HEADS - 1:
            j = h // N_KV_HEADS
            o_ref[0, :, j * LANES:(j + 1) * LANES] = (
                jnp.where(row0, outs[0], outs[1]).T.astype(BF16))
            outs = []


def _attention(q, k, vt):
    b, l, _ = q.shape
    tq = ATTN_TQ
    return pl.pallas_call(
        _attn_kernel,
        grid=(b, l // tq),
        in_specs=[pl.BlockSpec((1, tq, Q_W), lambda i, j: (i, j, 0)),
                  pl.BlockSpec((1, l, KV_W), lambda i, j: (i, 0, 0)),
                  pl.BlockSpec((1, KV_W, l), lambda i, j: (i, 0, 0))],
        out_specs=pl.BlockSpec((1, tq, Q_W), lambda i, j: (i, j, 0)),
        out_shape=jax.ShapeDtypeStruct((b, l, Q_W), BF16),
        compiler_params=_cparams("parallel", "parallel"),
        name="attention",
    )(q, k, vt)


def _ssm_kernel(uf_ref, ub_ref, bmat_ref, cmat_ref, a_ref, yf_ref, yb_ref, buf, carry):
    t_len = uf_ref.shape[1]
    c = pl.program_id(1)

    @pl.when(c == 0)
    def _():
        carry[...] = jnp.zeros_like(carry)

    nt = SSM_LANES // LANES
    for s in range(SSM_B):
        rows = slice(s * SSM_S, s * SSM_S + t_len)
        for d, u_ref in enumerate((uf_ref, ub_ref)):
            bu = _dot(u_ref[s], bmat_ref[d])
            for c in range(2 * nt):
                buf[d * 2 * nt + c, rows, :] = bu[:, c * LANES:(c + 1) * LANES]

    def step(i, st):
        new = []
        for d in range(2):
            t = i if d == 0 else t_len - 1 - i
            rows = pl.ds(t, SSM_B, stride=SSM_S)
            for c in range(nt):
                pr, pi = d * 2 * nt + c, d * 2 * nt + nt + c
                lanes = slice(c * LANES, (c + 1) * LANES)
                ar, ai = a_ref[2 * d, :, lanes], a_ref[2 * d + 1, :, lanes]
                sr, si = st[(d * nt + c) * 2], st[(d * nt + c) * 2 + 1]
                nr = ar * sr - ai * si + buf[pr, rows, :]
                ni = ar * si + ai * sr + buf[pi, rows, :]
                buf[pr, rows, :] = nr
                buf[pi, rows, :] = ni
                new += [nr, ni]
        return tuple(new)

    init = tuple(carry[n] for n in range(4 * nt))
    st = lax.fori_loop(0, t_len, step, init)
    for n in range(4 * nt):
        carry[n] = st[n]

    for s in range(SSM_B):
        rows = slice(s * SSM_S, s * SSM_S + t_len)
        for d, y_ref in enumerate((yf_ref, yb_ref)):
            xs = jnp.concatenate([buf[d * 2 * nt + c, rows, :] for c in range(2 * nt)],
                                 axis=1).astype(BF16)
            y_ref[s] = _dot(xs, cmat_ref[d]).astype(y_ref.dtype)


def _ssm_scan(u, bmat, cmat, a_vec):
    b, l, w = u.shape
    t = SSM_T
    nc = l // t
    blk = (SSM_B, t, w)
    planes = 4 * SSM_LANES // LANES
    out = jax.ShapeDtypeStruct((b, l, w), BF16)
    return pl.pallas_call(
        _ssm_kernel,
        grid=(b // SSM_B, nc),
        in_specs=[pl.BlockSpec(blk, lambda i, c: (i, c, 0)),
                  pl.BlockSpec(blk, lambda i, c: (i, nc - 1 - c, 0)),
                  _const_spec(bmat.shape), _const_spec(cmat.shape), _const_spec(a_vec.shape)],
        out_specs=[pl.BlockSpec(blk, lambda i, c: (i, c, 0)),
                   pl.BlockSpec(blk, lambda i, c: (i, nc - 1 - c, 0))],
        out_shape=[out, out],
        scratch_shapes=[pltpu.VMEM((planes, SSM_B * SSM_S, LANES), F32),
                        pltpu.VMEM((planes, SSM_B, LANES), F32)],
        compiler_params=_cparams("parallel", "arbitrary"),
        name="ssm_scan",
    )(u, u, bmat, cmat, a_vec)


def _first_index_of_max(vals, idx, n, axis):
    m = jnp.max(vals, axis=axis, keepdims=True)
    return jnp.min(jnp.where(vals == m, idx, n), axis=axis, keepdims=True)


def _route(scores, bias):
    tm = scores.shape[1]
    neg = -jnp.inf
    biased = scores + bias
    b3 = biased.reshape(N_EXPERT_GROUPS, EXPERTS_PER_GROUP, tm)
    j3 = lax.broadcasted_iota(jnp.int32, b3.shape, 1)
    m1 = jnp.max(b3, axis=1, keepdims=True)
    f1 = jnp.min(jnp.where(b3 == m1, j3, EXPERTS_PER_GROUP), axis=1, keepdims=True)
    m2 = jnp.max(jnp.where(j3 == f1, neg, b3), axis=1, keepdims=True)
    gscore = (m1 + m2)[:, 0, :]
    gi = lax.broadcasted_iota(jnp.int32, gscore.shape, 0)
    gsel = jnp.zeros(gscore.shape, F32)
    cur = gscore
    for _ in range(TOPK_GROUPS):
        pick = gi == _first_index_of_max(cur, gi, N_EXPERT_GROUPS, 0)
        gsel = jnp.where(pick, 1.0, gsel)
        cur = jnp.where(pick, neg, cur)
    emask = jnp.broadcast_to(gsel[:, None, :], b3.shape) > 0.0
    masked = jnp.where(emask, b3, -1e30).reshape(N_EXPERTS, tm)
    ei = lax.broadcasted_iota(jnp.int32, masked.shape, 0)
    top_w = jnp.zeros(masked.shape, F32)
    cur = masked
    for _ in range(TOP_K):
        pick = ei == _first_index_of_max(cur, ei, N_EXPERTS, 0)
        top_w = jnp.where(pick, scores, top_w)
        cur = jnp.where(pick, neg, cur)
    return top_w / jnp.sum(top_w, axis=0, keepdims=True) * ROUTED_SCALE


def _merge_kernel(attn_ref, yf_ref, yb_ref, u_ref, ga_ref, gs_ref, x_ref,
                  dskip_ref, wglu_ref, bglu_ref, wau_ref, wsu_ref, wout_ref, gffn_ref,
                  wrh_ref, wrl_ref, rbias_ref, wsg_ref, wsup_ref, wsd_ref,
                  h_ref, xn_ref, gates_ref):
    y = (dskip_ref[...] * u_ref[...].astype(F32) + yf_ref[...].astype(F32)
         + yb_ref[...].astype(F32))
    y = _gelu_tanh(y)
    ssm = y * _sigmoid(_dot(y.astype(BF16), wglu_ref[...]) + bglu_ref[...])
    merged = (ga_ref[...].astype(F32) * _dot(attn_ref[...], wau_ref[...])
              + gs_ref[...].astype(F32) * _dot(ssm.astype(BF16), wsu_ref[...]))
    h = x_ref[...] + _dot(merged.astype(BF16), wout_ref[...])
    xn = _rms(h, gffn_ref[...])
    xh, xl = _split_bf16(xn)
    xn_ref[...] = xh
    sh = _silu(_dot(xh, wsg_ref[...])) * _dot(xh, wsup_ref[...])
    h_ref[...] = h + _dot(sh.astype(BF16), wsd_ref[...])
    wh = wrh_ref[...]
    logits = _dot_nt(wh, xh) + _dot_nt(wh, xl) + _dot_nt(wrl_ref[...], xh)
    gates_ref[...] = _route(_sigmoid(logits), rbias_ref[...]).T


def _merge(attn, yf, yb, u, ga, gs, x2, dskip, wglu, bglu, wau, wsu, wout, gffn, wrh, wrl, rbias,
           wsg, wsup, wsd):
    n, d = x2.shape
    tm = MERGE_TM
    row = lambda w: pl.BlockSpec((tm, w), lambda i: (i, 0))
    consts = (dskip, wglu, bglu, wau, wsu, wout, gffn, wrh, wrl, rbias, wsg, wsup, wsd)
    return pl.pallas_call(
        _merge_kernel,
        grid=(n // tm,),
        in_specs=[row(Q_W), row(SSM_WIDTH), row(SSM_WIDTH), row(SSM_WIDTH), row(d), row(d), row(d)]
                 + [_const_spec(c.shape) for c in consts],
        out_specs=[row(d), row(d), row(N_EXPERTS)],
        out_shape=[jax.ShapeDtypeStruct((n, d), F32), jax.ShapeDtypeStruct((n, d), BF16),
                   jax.ShapeDtypeStruct((n, N_EXPERTS), F32)],
        compiler_params=_cparams("parallel"),
        name="merge",
    )(attn, yf, yb, u, ga, gs, x2, *consts)


def _moe_kernel(x_ref, gates_ref, wg_ref, wu_ref, wd_ref, o_ref, acc_ref):
    j = pl.program_id(1)

    @pl.when(j == 0)
    def _():
        acc_ref[...] = jnp.zeros_like(acc_ref)

    x = x_ref[...]
    g = gates_ref[0]
    hid = _silu(_dot(x, wg_ref[...])) * _dot(x, wu_ref[...])
    hid = jnp.concatenate(
        [hid[:, e * EXPERT_DIM:(e + 1) * EXPERT_DIM] * g[:, e:e + 1] for e in range(MOE_EC)],
        axis=1).astype(BF16)
    for c0 in range(0, x.shape[1], MOE_NB):
        cols = slice(c0, c0 + MOE_NB)
        acc_ref[:, cols] += _dot(hid, wd_ref[:, cols])

    @pl.when(j == pl.num_programs(1) - 1)
    def _():
        o_ref[...] = acc_ref[...].astype(o_ref.dtype)


def _moe(xn, gates, wg, wu, wd):
    n, d = xn.shape
    tm = MOE_TM
    ew = MOE_EC * EXPERT_DIM
    row = pl.BlockSpec((tm, d), lambda i, j: (i, 0))
    return pl.pallas_call(
        _moe_kernel,
        grid=(n // tm, N_EXPERTS // MOE_EC),
        in_specs=[row,
                  pl.BlockSpec((1, tm, MOE_EC), lambda i, j: (j, i, 0)),
                  pl.BlockSpec((d, ew), lambda i, j: (0, j)),
                  pl.BlockSpec((d, ew), lambda i, j: (0, j)),
                  pl.BlockSpec((ew, d), lambda i, j: (j, 0))],
        out_specs=row,
        out_shape=jax.ShapeDtypeStruct((n, d), BF16),
        scratch_shapes=[pltpu.VMEM((tm, d), F32)],
        compiler_params=_cparams("parallel", "arbitrary"),
        name="moe",
    )(xn, gates, wg, wu, wd)


def _ple_kernel(h_ref, r_ref, p_ref, gple_ref, wpg_ref, wpp_ref, gfin_ref, o_ref, *, final):
    h = h_ref[...] + r_ref[...].astype(F32)
    gate = _sigmoid(_dot(_rms(h, gple_ref[...]).astype(BF16), wpg_ref[...]))
    h = h + gate * _dot(p_ref[...].astype(BF16), wpp_ref[...])
    o_ref[...] = _rms(h, gfin_ref[...]) if final else h


def _ple(h, routed, p2, gple, wpg, wpp, gfin, final):
    n, d = h.shape
    tm = PLE_TM
    row = lambda w: pl.BlockSpec((tm, w), lambda i: (i, 0))
    consts = (gple, wpg, wpp, gfin)
    return pl.pallas_call(
        functools.partial(_ple_kernel, final=final),
        grid=(n // tm,),
        in_specs=[row(d), row(d), row(p2.shape[1])] + [_const_spec(c.shape) for c in consts],
        out_specs=row(d),
        out_shape=jax.ShapeDtypeStruct((n, d), F32),
        compiler_params=_cparams("parallel"),
        name="ple",
    )(h, routed, p2, *consts)


def _rope_tables(seq_len):
    rows = seq_len // GRID_W
    row_ids = jnp.repeat(jnp.arange(rows, dtype=F32), GRID_W)
    col_ids = jnp.tile(jnp.arange(GRID_W, dtype=F32), rows)
    inv_freq = ROPE_THETA ** (-jnp.arange(0, ROT_HALF, 2, dtype=F32) / ROT_HALF)
    ra = row_ids[:, None] * inv_freq[None, :]
    ca = col_ids[:, None] * inv_freq[None, :]
    cos = jnp.concatenate([jnp.cos(ra), jnp.cos(ra), jnp.cos(ca), jnp.cos(ca)], axis=1)
    sin = jnp.concatenate([-jnp.sin(ra), jnp.sin(ra), -jnp.sin(ca), jnp.sin(ca)], axis=1)
    reps = LANES // HEAD_DIM
    return jnp.tile(cos, (1, reps)), jnp.tile(sin, (1, reps))


def _block_diag(t, eye):
    n_dir, g, a, b = t.shape
    return jnp.einsum('dgab,gh->dgahb', t, eye).reshape(n_dir, g * a, g * b)


def _layer(h2, p2, seq_len, final, prm, cos_t, sin_t):
    (g_mix, w_in, q_norm, k_norm, w_attn_up, a_re, a_im, log_dt, b_re, b_im, c_re, c_im,
     d_skip, w_glu, b_glu, w_ssm_up, w_out, g_ffn, w_router, router_bias, w_exp_gate,
     w_exp_up, w_exp_down, w_sh_gate, w_sh_up, w_sh_down, g_ple, w_ple_gate, w_ple_proj,
     g_final) = prm
    n, d = h2.shape
    bsz = n // seq_len
    row = lambda t: t.reshape(1, -1).astype(F32)

    rep = N_HEADS // N_KV_HEADS
    order = [g * rep + j for j in range(rep) for g in range(N_KV_HEADS)]
    qcols = jnp.concatenate([jnp.arange(HEAD_DIM) + hd * HEAD_DIM for hd in order])
    w_cat = jnp.concatenate([w_in[:, :Q_W][:, qcols], w_in[:, Q_W:]], axis=1).astype(BF16)
    reps = LANES // HEAD_DIM
    qg = jnp.tile(row(q_norm), (1, reps))
    kg = jnp.tile(row(k_norm), (1, reps))
    li = jnp.arange(LANES)
    ones_bd = (li[:, None] // HEAD_DIM == li[None, :] // HEAD_DIM).astype(BF16)

    q, k, vt, u, ga, gs = _proj(h2, row(g_mix), w_cat, qg, kg, cos_t, sin_t, ones_bd, seq_len)

    attn = _attention(q.reshape(bsz, seq_len, Q_W), k.reshape(bsz, seq_len, KV_W),
                      vt).reshape(n, Q_W)

    abar_re, abar_im, bbar_re, bbar_im = _ssm_prep(a_re, a_im, log_dt, b_re, b_im)
    eye = jnp.eye(SSM_GROUPS, dtype=F32)
    tr = lambda t: jnp.swapaxes(t, 2, 3)
    bmat = jnp.concatenate([_block_diag(tr(bbar_re), eye), _block_diag(tr(bbar_im), eye)],
                           axis=2).astype(BF16)
    cmat = jnp.concatenate([_block_diag(tr(c_re.astype(F32)), eye),
                            _block_diag(tr(-c_im.astype(F32)), eye)], axis=1).astype(BF16)
    a_vec = jnp.stack([abar_re[0], abar_im[0], abar_re[1], abar_im[1]]).reshape(4, 1, SSM_LANES)
    a_vec = jnp.broadcast_to(a_vec, (4, SSM_B, SSM_LANES))
    yf, yb = _ssm_scan(u.reshape(bsz, seq_len, SSM_WIDTH), bmat, cmat, a_vec)

    wr_t = w_router.T.astype(F32)
    wrh, wrl = _split_bf16(wr_t)
    h1, xn, gates = _merge(
        attn, yf.reshape(n, SSM_WIDTH), yb.reshape(n, SSM_WIDTH), u, ga, gs, h2,
        row(d_skip), w_glu.astype(BF16), row(b_glu), w_attn_up[qcols, :].astype(BF16),
        w_ssm_up.astype(BF16), w_out.astype(BF16), row(g_ffn), wrh, wrl,
        router_bias.reshape(-1, 1).astype(F32),
        w_sh_gate.astype(BF16), w_sh_up.astype(BF16), w_sh_down.astype(BF16))

    cat = lambda w: jnp.swapaxes(w, 0, 1).reshape(d, N_EXPERTS * EXPERT_DIM).astype(BF16)
    gates_by_step = jnp.swapaxes(gates.reshape(n, N_EXPERTS // MOE_EC, MOE_EC), 0, 1)
    routed = _moe(xn, gates_by_step, cat(w_exp_gate), cat(w_exp_up),
                  w_exp_down.reshape(N_EXPERTS * EXPERT_DIM, d).astype(BF16))

    return _ple(h1, routed, p2, row(g_ple), w_ple_gate.astype(BF16), w_ple_proj.astype(BF16),
                row(g_final), final)


def kernel(x, p, g_mix, w_in, q_norm, k_norm, w_attn_up, a_re, a_im, log_dt, b_re, b_im, c_re,
           c_im, d_skip, w_glu, b_glu, w_ssm_up, w_out, g_ffn, w_router, router_bias,
           w_exp_gate, w_exp_up, w_exp_down, w_sh_gate, w_sh_up, w_sh_down, g_ple,
           w_ple_gate, w_ple_proj, g_final):
    bsz, seq_len, d = x.shape
    depth = p.shape[0]
    stacked = (g_mix, w_in, q_norm, k_norm, w_attn_up, a_re, a_im, log_dt, b_re, b_im, c_re,
               c_im, d_skip, w_glu, b_glu, w_ssm_up, w_out, g_ffn, w_router, router_bias,
               w_exp_gate, w_exp_up, w_exp_down, w_sh_gate, w_sh_up, w_sh_down, g_ple,
               w_ple_gate, w_ple_proj)
    cos_t, sin_t = _rope_tables(seq_len)
    h = x.reshape(bsz * seq_len, d)
    for i in range(depth):
        prm = tuple(t[i] for t in stacked) + (g_final,)
        h = _layer(h, p[i].reshape(bsz * seq_len, -1), seq_len, i == depth - 1, prm,
                   cos_t, sin_t)
    return h.reshape(bsz, seq_len, d)
```

```python
import functools
import math

import jax
import jax.numpy as jnp
from jax import lax
from jax.experimental import pallas as pl
from jax.experimental.pallas import tpu as pltpu

F32 = jnp.float32
BF16 = jnp.bfloat16
F8 = jnp.float8_e4m3fn
F8_TARGET = 256.0
F8_TINY = 1e-30

N_HEADS = 8
N_KV_HEADS = 2
HEAD_DIM = 64
ROPE_THETA = 10000.0
GRID_W = 64
ROT_HALF = HEAD_DIM // 2
ROT_QUARTER = ROT_HALF // 2
SSM_WIDTH = 256
SSM_GROUP = 16
SSM_GROUPS = SSM_WIDTH // SSM_GROUP
SSM_STATE = 64
SSM_LANES = SSM_GROUPS * SSM_STATE
N_EXPERTS = 64
EXPERT_DIM = 128
TOP_K = 8
N_EXPERT_GROUPS = 8
TOPK_GROUPS = 4
EXPERTS_PER_GROUP = N_EXPERTS // N_EXPERT_GROUPS
ROUTED_SCALE = 2.5
EPS = 1e-6
Q_W = N_HEADS * HEAD_DIM
KV_W = N_KV_HEADS * HEAD_DIM

LANES = 128
SUBLANES = 8
VMEM_LIMIT = 56 * 1024 * 1024

PROJ_TM = 512
ATTN_TQ = 256
ATTN_KC = 256
SSM_T = 128
SSM_B = SUBLANES
SSM_S = SSM_T + SUBLANES
MERGE_TM = 512
MOE_TM = 1024
MOE_EC = 8
MOE_NB = 256
PLE_TM = 512


def _cparams(*sem):
    return pltpu.CompilerParams(dimension_semantics=sem, vmem_limit_bytes=VMEM_LIMIT)


def _const_spec(shape):
    nd = len(shape)
    return pl.BlockSpec(shape, lambda *_: (0,) * nd)


def _dot(a, b):
    return jnp.dot(a, b, preferred_element_type=F32)


def _dot_nt(a, b):
    return lax.dot_general(a, b, (((1,), (1,)), ((), ())), preferred_element_type=F32)


def _split_bf16(x):
    hi = x.astype(BF16)
    lo = (x - hi.astype(F32)).astype(BF16)
    return hi, lo


def _rms(x, g):
    ms = jnp.mean(x * x, axis=-1, keepdims=True)
    return x * lax.rsqrt(ms + EPS) * g


def _sigmoid(x):
    return 1.0 / (1.0 + jnp.exp(-x))


def _silu(x):
    return x * _sigmoid(x)


def _gelu_tanh(x):
    c = math.sqrt(2.0 / math.pi)
    return 0.5 * x * (1.0 + jnp.tanh(c * (x + 0.044715 * (x * x * x))))


def _ssm_prep_kernel(are, aim, ldt, bre, bim, oar, oai, obr, obi):
    dt = jnp.exp(ldt[...])
    lr = are[...]
    li = aim[...]
    mag = jnp.exp(lr * dt)
    ar = mag * jnp.cos(li * dt)
    ai = mag * jnp.sin(li * dt)
    xr = ar - 1.0
    den = lr * lr + li * li
    qr = (xr * lr + ai * li) / den
    qi = (ai * lr - xr * li) / den
    oar[...] = ar
    oai[...] = ai
    obr[...] = qr * bre[...] - qi * bim[...]
    obi[...] = qr * bim[...] + qi * bre[...]


def _ssm_prep(a_re, a_im, log_dt, b_re, b_im):
    n_dir = a_re.shape[0]
    rows = n_dir * SSM_GROUPS * SSM_STATE
    full = (n_dir, SSM_GROUPS, SSM_STATE, SSM_GROUP)
    bc = lambda t: jnp.broadcast_to(t, full).reshape(rows, SSM_GROUP).astype(F32)
    args = (bc(a_re[..., None]), bc(a_im[..., None]), bc(log_dt[..., None, None]),
            b_re.reshape(rows, SSM_GROUP).astype(F32), b_im.reshape(rows, SSM_GROUP).astype(F32))
    shp = jax.ShapeDtypeStruct((rows, SSM_GROUP), F32)
    oar, oai, obr, obi = pl.pallas_call(
        _ssm_prep_kernel, out_shape=(shp, shp, shp, shp), name="ssm_prep")(*args)
    rs = lambda t: t.reshape(full)
    return rs(oar)[..., 0], rs(oai)[..., 0], rs(obr), rs(obi)


def _proj_kernel(x_ref, g_ref, w_ref, qg_ref, kg_ref, cos_ref, sin_ref, ones_ref,
                 q_ref, k_ref, vt_ref, u_ref, ga_ref, gs_ref):
    tm = x_ref.shape[0]
    xn = _rms(x_ref[...], g_ref[...]).astype(BF16)
    proj = _dot(xn, w_ref[...])
    cos = cos_ref[...]
    sin = sin_ref[...]
    ones = ones_ref[...]
    lane = lax.broadcasted_iota(jnp.int32, (tm, LANES), 1)
    first = (lane % ROT_HALF) < ROT_QUARTER

    def norm_rope(t, gain):
        hi, lo = _split_bf16(t * t)
        ss = _dot(hi, ones) + _dot(lo, ones)
        tn = t * lax.rsqrt(ss * (1.0 / HEAD_DIM) + EPS) * gain
        partner = jnp.where(first, pltpu.roll(tn, LANES - ROT_QUARTER, 1),
                            pltpu.roll(tn, ROT_QUARTER, 1))
        return tn * cos + partner * sin

    scale = math.log2(math.e) / math.sqrt(HEAD_DIM)
    for j in range(Q_W // LANES):
        sl = slice(j * LANES, (j + 1) * LANES)
        q_ref[:, sl] = (norm_rope(proj[:, sl], qg_ref[...]) * scale).astype(BF16)
    o = Q_W
    k_ref[...] = norm_rope(proj[:, o:o + KV_W], kg_ref[...]).astype(BF16)
    o += KV_W
    vt_ref[0, :KV_W, :] = proj[:, o:o + KV_W].T.astype(BF16)
    vt_ref[0, KV_W:, :] = jnp.ones((ONES_ROWS, tm), BF16)
    o += KV_W
    u_ref[...] = proj[:, o:o + SSM_WIDTH].astype(BF16)
    o += SSM_WIDTH
    d = ga_ref.shape[1]
    ga_ref[...] = _sigmoid(proj[:, o:o + d]).astype(BF16)
    o += d
    gs_ref[...] = _sigmoid(proj[:, o:o + d]).astype(BF16)


def _proj(x2, g_mix, w_cat, qg, kg, cos_t, sin_t, ones_bd, seq_len):
    n, d = x2.shape
    tm = PROJ_TM
    per_seq = seq_len // tm
    row = lambda w: pl.BlockSpec((tm, w), lambda i: (i, 0))
    pos = pl.BlockSpec((tm, LANES), lambda i: (i % per_seq, 0))
    out = lambda w: jax.ShapeDtypeStruct((n, w), BF16)
    vt_rows = KV_W + ONES_ROWS
    vt_spec = pl.BlockSpec((1, vt_rows, tm), lambda i: (i // per_seq, 0, i % per_seq))
    vt_shape = jax.ShapeDtypeStruct((n // seq_len, vt_rows, seq_len), BF16)
    return pl.pallas_call(
        _proj_kernel,
        grid=(n // tm,),
        in_specs=[row(d), _const_spec(g_mix.shape), _const_spec(w_cat.shape),
                  _const_spec(qg.shape), _const_spec(kg.shape), pos, pos,
                  _const_spec(ones_bd.shape)],
        out_specs=[row(Q_W), row(KV_W), vt_spec, row(SSM_WIDTH), row(d), row(d)],
        out_shape=[out(Q_W), out(KV_W), vt_shape, out(SSM_WIDTH), out(d), out(d)],
        compiler_params=_cparams("parallel"),
        name="proj",
    )(x2, g_mix, w_cat, qg, kg, cos_t, sin_t, ones_bd)


ONES_ROWS = 16


def _attn_kernel(q_ref, k_ref, vt_ref, o_ref, s0_ref, s1_ref, p0_ref, p1_ref):
    s_refs, p_refs = (s0_ref, s1_ref), (p0_ref, p1_ref)
    tq = q_ref.shape[1]
    seq = k_ref.shape[1]
    lane = lax.broadcasted_iota(jnp.int32, (tq, LANES), 1)
    group0 = lane < HEAD_DIM
    row0 = lax.broadcasted_iota(jnp.int32, (KV_W, tq), 0) < HEAD_DIM
    zero = jnp.zeros((tq, LANES), BF16)
    chunks = [slice(c0, c0 + ATTN_KC) for c0 in range(0, seq, ATTN_KC)]

    def masked_q(h):
        j, g = divmod(h, N_KV_HEADS)
        qc = q_ref[0, :, j * LANES:(j + 1) * LANES]
        return jnp.where(group0 if g == 0 else jnp.logical_not(group0), qc, zero)

    def score_chunk(h, qm, kc):
        sc = _dot_nt(k_ref[0, kc, :], qm)
        s_refs[h % 2][kc, :] = sc
        return jnp.max(sc, axis=0, keepdims=True)

    def exp_chunk(h, m, kc):
        p_refs[h % 2][kc, :] = jnp.exp2((s_refs[h % 2][kc, :] - m).astype(BF16))

    def value_chunk(h, kc):
        return _dot(vt_ref[0, :, kc], p_refs[h % 2][kc, :])

    maxes, outs = {}, []
    for t in range(N_HEADS + 2):
        hq, he, hv = t, t - 1, t - 2
        if hq < N_HEADS:
            qm = masked_q(hq)
        r, m_new = None, None
        for kc in chunks:
            if hq < N_HEADS:
                mc = score_chunk(hq, qm, kc)
                m_new = mc if m_new is None else jnp.maximum(m_new, mc)
            if 0 <= he < N_HEADS:
                exp_chunk(he, maxes[he], kc)
            if hv >= 0:
                rc = value_chunk(hv, kc)
                r = rc if r is None else r + rc
        maxes[hq] = m_new
        if hv >= 0:
            outs.append(r[:KV_W] / r[KV_W:KV_W + 1])
            if hv % N_KV_HEADS == N_KV_HEADS - 1:
                j = hv // N_KV_HEADS
                o_ref[0, :, j * LANES:(j + 1) * LANES] = (
                    jnp.where(row0, outs[0], outs[1]).T.astype(BF16))
                outs = []


def _attention(q, k, vt):
    b, l, _ = q.shape
    tq = ATTN_TQ
    return pl.pallas_call(
        _attn_kernel,
        grid=(b, l // tq),
        in_specs=[pl.BlockSpec((1, tq, Q_W), lambda i, j: (i, j, 0)),
                  pl.BlockSpec((1, l, KV_W), lambda i, j: (i, 0, 0)),
                  pl.BlockSpec((1, KV_W + ONES_ROWS, l), lambda i, j: (i, 0, 0))],
        out_specs=pl.BlockSpec((1, tq, Q_W), lambda i, j: (i, j, 0)),
        out_shape=jax.ShapeDtypeStruct((b, l, Q_W), BF16),
        scratch_shapes=[pltpu.VMEM((l, tq), F32), pltpu.VMEM((l, tq), F32),
                        pltpu.VMEM((l, tq), BF16), pltpu.VMEM((l, tq), BF16)],
        compiler_params=_cparams("parallel", "parallel"),
        name="attention",
    )(q, k, vt)


def _ssm_kernel(uf_ref, ub_ref, bmat_ref, cmat_ref, a_ref, yf_ref, yb_ref, buf, carry):
    t_len = uf_ref.shape[1]
    c = pl.program_id(1)

    @pl.when(c == 0)
    def _():
        carry[...] = jnp.zeros_like(carry)

    nt = SSM_LANES // LANES
    for s in range(SSM_B):
        rows = slice(s * SSM_S, s * SSM_S + t_len)
        for d, u_ref in enumerate((uf_ref, ub_ref)):
            bu = _dot(u_ref[s], bmat_ref[d])
            for c in range(2 * nt):
                buf[d * 2 * nt + c, rows, :] = bu[:, c * LANES:(c + 1) * LANES]

    for d in range(2):
        coef = [(a_ref[2 * d, :, c * LANES:(c + 1) * LANES],
                 a_ref[2 * d + 1, :, c * LANES:(c + 1) * LANES]) for c in range(nt)]

        def step(i, st, d=d, coef=coef):
            t = i if d == 0 else t_len - 1 - i
            rows = pl.ds(t, SSM_B, stride=SSM_S)
            new = []
            for c in range(nt):
                pr, pi = d * 2 * nt + c, d * 2 * nt + nt + c
                ar, ai = coef[c]
                sr, si = st[2 * c], st[2 * c + 1]
                nr = ar * sr - ai * si + buf[pr, rows, :]
                ni = ar * si + ai * sr + buf[pi, rows, :]
                buf[pr, rows, :] = nr
                buf[pi, rows, :] = ni
                new += [nr, ni]
            return tuple(new)

        base = d * 2 * nt
        st = lax.fori_loop(0, t_len, step, tuple(carry[base + n] for n in range(2 * nt)))
        for n in range(2 * nt):
            carry[base + n] = st[n]

    for s in range(SSM_B):
        rows = slice(s * SSM_S, s * SSM_S + t_len)
        for d, y_ref in enumerate((yf_ref, yb_ref)):
            xs = jnp.concatenate([buf[d * 2 * nt + c, rows, :] for c in range(2 * nt)],
                                 axis=1).astype(BF16)
            y_ref[s] = _dot(xs, cmat_ref[d]).astype(y_ref.dtype)


def _ssm_scan(u, bmat, cmat, a_vec):
    b, l, w = u.shape
    t = SSM_T
    nc = l // t
    blk = (SSM_B, t, w)
    planes = 4 * SSM_LANES // LANES
    out = jax.ShapeDtypeStruct((b, l, w), BF16)
    return pl.pallas_call(
        _ssm_kernel,
        grid=(b // SSM_B, nc),
        in_specs=[pl.BlockSpec(blk, lambda i, c: (i, c, 0)),
                  pl.BlockSpec(blk, lambda i, c: (i, nc - 1 - c, 0)),
                  _const_spec(bmat.shape), _const_spec(cmat.shape), _const_spec(a_vec.shape)],
        out_specs=[pl.BlockSpec(blk, lambda i, c: (i, c, 0)),
                   pl.BlockSpec(blk, lambda i, c: (i, nc - 1 - c, 0))],
        out_shape=[out, out],
        scratch_shapes=[pltpu.VMEM((planes, SSM_B * SSM_S, LANES), F32),
                        pltpu.VMEM((planes, SSM_B, LANES), F32)],
        compiler_params=_cparams("parallel", "arbitrary"),
        name="ssm_scan",
    )(u, u, bmat, cmat, a_vec)


def _first_index_of_max(vals, idx, n, axis):
    m = jnp.max(vals, axis=axis, keepdims=True)
    return jnp.min(jnp.where(vals == m, idx, n), axis=axis, keepdims=True)


def _route(scores, bias):
    tm = scores.shape[1]
    neg = -jnp.inf
    biased = scores + bias
    b3 = biased.reshape(N_EXPERT_GROUPS, EXPERTS_PER_GROUP, tm)
    j3 = lax.broadcasted_iota(jnp.int32, b3.shape, 1)
    m1 = jnp.max(b3, axis=1, keepdims=True)
    f1 = jnp.min(jnp.where(b3 == m1, j3, EXPERTS_PER_GROUP), axis=1, keepdims=True)
    m2 = jnp.max(jnp.where(j3 == f1, neg, b3), axis=1, keepdims=True)
    gscore = (m1 + m2)[:, 0, :]
    gi = lax.broadcasted_iota(jnp.int32, gscore.shape, 0)
    gsel = jnp.zeros(gscore.shape, F32)
    cur = gscore
    for _ in range(TOPK_GROUPS):
        pick = gi == _first_index_of_max(cur, gi, N_EXPERT_GROUPS, 0)
        gsel = jnp.where(pick, 1.0, gsel)
        cur = jnp.where(pick, neg, cur)
    emask = jnp.broadcast_to(gsel[:, None, :], b3.shape) > 0.0
    masked = jnp.where(emask, b3, -1e30).reshape(N_EXPERTS, tm)
    ei = lax.broadcasted_iota(jnp.int32, masked.shape, 0)
    top_w = jnp.zeros(masked.shape, F32)
    cur = masked
    for _ in range(TOP_K):
        pick = ei == _first_index_of_max(cur, ei, N_EXPERTS, 0)
        top_w = jnp.where(pick, scores, top_w)
        cur = jnp.where(pick, neg, cur)
    return top_w / jnp.sum(top_w, axis=0, keepdims=True) * ROUTED_SCALE


def _merge_kernel(attn_ref, yf_ref, yb_ref, u_ref, ga_ref, gs_ref, x_ref,
                  dskip_ref, wglu_ref, bglu_ref, wau_ref, wsu_ref, wout_ref, gffn_ref,
                  wrh_ref, wrl_ref, rbias_ref, wsg_ref, wsup_ref, wsd_ref,
                  h_ref, xn_ref, gates_ref):
    y = (dskip_ref[...] * u_ref[...].astype(F32) + yf_ref[...].astype(F32)
         + yb_ref[...].astype(F32))
    y = _gelu_tanh(y)
    ssm = y * _sigmoid(_dot(y.astype(BF16), wglu_ref[...]) + bglu_ref[...])
    merged = (ga_ref[...].astype(F32) * _dot(attn_ref[...], wau_ref[...])
              + gs_ref[...].astype(F32) * _dot(ssm.astype(BF16), wsu_ref[...]))
    h = x_ref[...] + _dot(merged.astype(BF16), wout_ref[...])
    xn = _rms(h, gffn_ref[...])
    xh, xl = _split_bf16(xn)
    xn_ref[...] = xh
    sh = _silu(_dot(xh, wsg_ref[...])) * _dot(xh, wsup_ref[...])
    h_ref[...] = h + _dot(sh.astype(BF16), wsd_ref[...])
    wh = wrh_ref[...]
    logits = _dot_nt(wh, xh) + _dot_nt(wh, xl) + _dot_nt(wrl_ref[...], xh)
    gates_ref[...] = _route(_sigmoid(logits), rbias_ref[...]).T


def _merge(attn, yf, yb, u, ga, gs, x2, dskip, wglu, bglu, wau, wsu, wout, gffn, wrh, wrl, rbias,
           wsg, wsup, wsd):
    n, d = x2.shape
    tm = MERGE_TM
    row = lambda w: pl.BlockSpec((tm, w), lambda i: (i, 0))
    consts = (dskip, wglu, bglu, wau, wsu, wout, gffn, wrh, wrl, rbias, wsg, wsup, wsd)
    return pl.pallas_call(
        _merge_kernel,
        grid=(n // tm,),
        in_specs=[row(Q_W), row(SSM_WIDTH), row(SSM_WIDTH), row(SSM_WIDTH), row(d), row(d), row(d)]
                 + [_const_spec(c.shape) for c in consts],
        out_specs=[row(d), row(d), row(N_EXPERTS)],
        out_shape=[jax.ShapeDtypeStruct((n, d), F32), jax.ShapeDtypeStruct((n, d), BF16),
                   jax.ShapeDtypeStruct((n, N_EXPERTS), F32)],
        compiler_params=_cparams("parallel"),
        name="merge",
    )(attn, yf, yb, u, ga, gs, x2, *consts)


def _pow2_scale(max_abs):
    return jnp.exp2(jnp.floor(jnp.log2(F8_TARGET / jnp.maximum(max_abs, F8_TINY))))


def _row_max_abs(t):
    return jnp.max(jnp.abs(t), axis=1, keepdims=True)


def _moe_kernel(inv_ref, x_ref, gates_ref, wg_ref, wu_ref, wd_ref, o_ref, acc_ref, x8_ref, sx_ref):
    j = pl.program_id(1)

    @pl.when(j == 0)
    def _():
        acc_ref[...] = jnp.zeros_like(acc_ref)
        x = x_ref[...].astype(F32)
        sx = _pow2_scale(_row_max_abs(x))
        x8_ref[...] = (x * sx).astype(F8)
        sx_ref[...] = 1.0 / sx

    x8 = x8_ref[...]
    inv_sx = sx_ref[...]
    gate_pre = _dot(x8, wg_ref[...]).astype(BF16) * (inv_sx * inv_ref[0]).astype(BF16)
    g = (gates_ref[0] * (inv_sx * inv_ref[1])).astype(BF16)
    hid = _silu(gate_pre) * _dot(x8, wu_ref[...]).astype(BF16)
    hid = jnp.concatenate(
        [hid[:, e * EXPERT_DIM:(e + 1) * EXPERT_DIM] * g[:, e:e + 1] for e in range(MOE_EC)],
        axis=1)
    sh = _pow2_scale(_row_max_abs(hid).astype(F32))
    hid8 = (hid * sh.astype(BF16)).astype(F8)
    unscale = inv_ref[2] / sh
    for c0 in range(0, hid.shape[1], MOE_NB):
        cols = slice(c0, c0 + MOE_NB)
        acc_ref[:, cols] += _dot(hid8, wd_ref[:, cols]) * unscale

    @pl.when(j == pl.num_programs(1) - 1)
    def _():
        o_ref[...] = acc_ref[...].astype(o_ref.dtype)


def _to_f8(w):
    s = _pow2_scale(jnp.max(jnp.abs(w)))
    return (w * s).astype(F8), 1.0 / s


def _moe(xn, gates, w_gate, w_up, w_down):
    n, d = xn.shape
    tm = MOE_TM
    ew = MOE_EC * EXPERT_DIM
    wg, inv_g = _to_f8(w_gate)
    wu, inv_u = _to_f8(w_up)
    wd, inv_d = _to_f8(w_down)
    inv = jnp.stack([inv_g, inv_u, inv_d]).astype(F32)
    row = pl.BlockSpec((tm, d), lambda i, j: (i, 0))
    return pl.pallas_call(
        _moe_kernel,
        grid=(n // tm, N_EXPERTS // MOE_EC),
        in_specs=[pl.BlockSpec(memory_space=pltpu.SMEM),
                  row,
                  pl.BlockSpec((1, tm, MOE_EC), lambda i, j: (j, i, 0)),
                  pl.BlockSpec((d, ew), lambda i, j: (0, j)),
                  pl.BlockSpec((d, ew), lambda i, j: (0, j)),
                  pl.BlockSpec((ew, d), lambda i, j: (j, 0))],
        out_specs=row,
        out_shape=jax.ShapeDtypeStruct((n, d), BF16),
        scratch_shapes=[pltpu.VMEM((tm, d), F32), pltpu.VMEM((tm, d), F8),
                        pltpu.VMEM((tm, 1), F32)],
        compiler_params=_cparams("parallel", "arbitrary"),
        name="moe",
    )(inv, xn, gates, wg, wu, wd)


def _ple_kernel(h_ref, r_ref, p_ref, gple_ref, wpg_ref, wpp_ref, gfin_ref, o_ref, *, final):
    h = h_ref[...] + r_ref[...].astype(F32)
    gate = _sigmoid(_dot(_rms(h, gple_ref[...]).astype(BF16), wpg_ref[...]))
    h = h + gate * _dot(p_ref[...].astype(BF16), wpp_ref[...])
    o_ref[...] = _rms(h, gfin_ref[...]) if final else h


def _ple(h, routed, p2, gple, wpg, wpp, gfin, final):
    n, d = h.shape
    tm = PLE_TM
    row = lambda w: pl.BlockSpec((tm, w), lambda i: (i, 0))
    consts = (gple, wpg, wpp, gfin)
    return pl.pallas_call(
        functools.partial(_ple_kernel, final=final),
        grid=(n // tm,),
        in_specs=[row(d), row(d), row(p2.shape[1])] + [_const_spec(c.shape) for c in consts],
        out_specs=row(d),
        out_shape=jax.ShapeDtypeStruct((n, d), F32),
        compiler_params=_cparams("parallel"),
        name="ple",
    )(h, routed, p2, *consts)


def _rope_tables(seq_len):
    rows = seq_len // GRID_W
    row_ids = jnp.repeat(jnp.arange(rows, dtype=F32), GRID_W)
    col_ids = jnp.tile(jnp.arange(GRID_W, dtype=F32), rows)
    inv_freq = ROPE_THETA ** (-jnp.arange(0, ROT_HALF, 2, dtype=F32) / ROT_HALF)
    ra = row_ids[:, None] * inv_freq[None, :]
    ca = col_ids[:, None] * inv_freq[None, :]
    cos = jnp.concatenate([jnp.cos(ra), jnp.cos(ra), jnp.cos(ca), jnp.cos(ca)], axis=1)
    sin = jnp.concatenate([-jnp.sin(ra), jnp.sin(ra), -jnp.sin(ca), jnp.sin(ca)], axis=1)
    reps = LANES // HEAD_DIM
    return jnp.tile(cos, (1, reps)), jnp.tile(sin, (1, reps))


def _block_diag(t, eye):
    n_dir, g, a, b = t.shape
    return jnp.einsum('dgab,gh->dgahb', t, eye).reshape(n_dir, g * a, g * b)


def _layer(h2, p2, seq_len, final, prm, cos_t, sin_t):
    (g_mix, w_in, q_norm, k_norm, w_attn_up, a_re, a_im, log_dt, b_re, b_im, c_re, c_im,
     d_skip, w_glu, b_glu, w_ssm_up, w_out, g_ffn, w_router, router_bias, w_exp_gate,
     w_exp_up, w_exp_down, w_sh_gate, w_sh_up, w_sh_down, g_ple, w_ple_gate, w_ple_proj,
     g_final) = prm
    n, d = h2.shape
    bsz = n // seq_len
    row = lambda t: t.reshape(1, -1).astype(F32)

    rep = N_HEADS // N_KV_HEADS
    order = [g * rep + j for j in range(rep) for g in range(N_KV_HEADS)]
    qcols = jnp.concatenate([jnp.arange(HEAD_DIM) + hd * HEAD_DIM for hd in order])
    w_cat = jnp.concatenate([w_in[:, :Q_W][:, qcols], w_in[:, Q_W:]], axis=1).astype(BF16)
    reps = LANES // HEAD_DIM
    qg = jnp.tile(row(q_norm), (1, reps))
    kg = jnp.tile(row(k_norm), (1, reps))
    li = jnp.arange(LANES)
    ones_bd = (li[:, None] // HEAD_DIM == li[None, :] // HEAD_DIM).astype(BF16)

    q, k, vt, u, ga, gs = _proj(h2, row(g_mix), w_cat, qg, kg, cos_t, sin_t, ones_bd, seq_len)

    attn = _attention(q.reshape(bsz, seq_len, Q_W), k.reshape(bsz, seq_len, KV_W),
                      vt).reshape(n, Q_W)

    abar_re, abar_im, bbar_re, bbar_im = _ssm_prep(a_re, a_im, log_dt, b_re, b_im)
    eye = jnp.eye(SSM_GROUPS, dtype=F32)
    tr = lambda t: jnp.swapaxes(t, 2, 3)
    bmat = jnp.concatenate([_block_diag(tr(bbar_re), eye), _block_diag(tr(bbar_im), eye)],
                           axis=2).astype(BF16)
    cmat = jnp.concatenate([_block_diag(tr(c_re.astype(F32)), eye),
                            _block_diag(tr(-c_im.astype(F32)), eye)], axis=1).astype(BF16)
    a_vec = jnp.stack([abar_re[0], abar_im[0], abar_re[1], abar_im[1]]).reshape(4, 1, SSM_LANES)
    a_vec = jnp.broadcast_to(a_vec, (4, SSM_B, SSM_LANES))
    yf, yb = _ssm_scan(u.reshape(bsz, seq_len, SSM_WIDTH), bmat, cmat, a_vec)

    wr_t = w_router.T.astype(F32)
    wrh, wrl = _split_bf16(wr_t)
    h1, xn, gates = _merge(
        attn, yf.reshape(n, SSM_WIDTH), yb.reshape(n, SSM_WIDTH), u, ga, gs, h2,
        row(d_skip), w_glu.astype(BF16), row(b_glu), w_attn_up[qcols, :].astype(BF16),
        w_ssm_up.astype(BF16), w_out.astype(BF16), row(g_ffn), wrh, wrl,
        router_bias.reshape(-1, 1).astype(F32),
        w_sh_gate.astype(BF16), w_sh_up.astype(BF16), w_sh_down.astype(BF16))

    cat = lambda w: jnp.swapaxes(w, 0, 1).reshape(d, N_EXPERTS * EXPERT_DIM).astype(F32)
    gates_by_step = jnp.swapaxes(gates.reshape(n, N_EXPERTS // MOE_EC, MOE_EC), 0, 1)
    routed = _moe(xn, gates_by_step, cat(w_exp_gate), cat(w_exp_up),
                  w_exp_down.reshape(N_EXPERTS * EXPERT_DIM, d).astype(F32))

    return _ple(h1, routed, p2, row(g_ple), w_ple_gate.astype(BF16), w_ple_proj.astype(BF16),
                row(g_final), final)


def kernel(x, p, g_mix, w_in, q_norm, k_norm, w_attn_up, a_re, a_im, log_dt, b_re, b_im, c_re,
           c_im, d_skip, w_glu, b_glu, w_ssm_up, w_out, g_ffn, w_router, router_bias,
           w_exp_gate, w_exp_up, w_exp_down, w_sh_gate, w_sh_up, w_sh_down, g_ple,
           w_ple_gate, w_ple_proj, g_final):
    bsz, seq_len, d = x.shape
    depth = p.shape[0]
    stacked = (g_mix, w_in, q_norm, k_norm, w_attn_up, a_re, a_im, log_dt, b_re, b_im, c_re,
               c_im, d_skip, w_glu, b_glu, w_ssm_up, w_out, g_ffn, w_router, router_bias,
               w_exp_gate, w_exp_up, w_exp_down, w_sh_gate, w_sh_up, w_sh_down, g_ple,
               w_ple_gate, w_ple_proj)
    cos_t, sin_t = _rope_tables(seq_len)
    h = x.reshape(bsz * seq_len, d)
    for i in range(depth):
        prm = tuple(t[i] for t in stacked) + (g_final,)
        h = _layer(h, p[i].reshape(bsz * seq_len, -1), seq_len, i == depth - 1, prm,
                   cos_t, sin_t)
    return h.reshape(bsz, seq_len, d)
```

```python
import functools
import math

import jax
import jax.numpy as jnp
from jax import lax
from jax.experimental import pallas as pl
from jax.experimental.pallas import tpu as pltpu

F32 = jnp.float32
BF16 = jnp.bfloat16
F8 = jnp.float8_e4m3fn
F8_TARGET = 256.0
F8_TINY = 1e-30

N_HEADS = 8
N_KV_HEADS = 2
HEAD_DIM = 64
ROPE_THETA = 10000.0
GRID_W = 64
ROT_HALF = HEAD_DIM // 2
ROT_QUARTER = ROT_HALF // 2
SSM_WIDTH = 256
SSM_GROUP = 16
SSM_GROUPS = SSM_WIDTH // SSM_GROUP
SSM_STATE = 64
SSM_LANES = SSM_GROUPS * SSM_STATE
N_EXPERTS = 64
EXPERT_DIM = 128
TOP_K = 8
N_EXPERT_GROUPS = 8
TOPK_GROUPS = 4
EXPERTS_PER_GROUP = N_EXPERTS // N_EXPERT_GROUPS
ROUTED_SCALE = 2.5
EPS = 1e-6
Q_W = N_HEADS * HEAD_DIM
KV_W = N_KV_HEADS * HEAD_DIM
Q_SCALE = math.log2(math.e) / math.sqrt(HEAD_DIM)

LANES = 128
SUBLANES = 8
VMEM_LIMIT = 56 * 1024 * 1024

PROJ_TM = 512
ATTN_TQ = 256
ATTN_KC = 256
SSM_T = 128
SSM_B = SUBLANES
SSM_S = SSM_T + SUBLANES
MERGE_TM = 512
MOE_TM = 1024
MOE_EC = 16
MOE_NB = 256
PLE_TM = 512


def _cparams(*sem):
    return pltpu.CompilerParams(dimension_semantics=sem, vmem_limit_bytes=VMEM_LIMIT)


def _const_spec(shape):
    nd = len(shape)
    return pl.BlockSpec(shape, lambda *_: (0,) * nd)


def _dot(a, b):
    return jnp.dot(a, b, preferred_element_type=F32)


def _dot_nt(a, b):
    return lax.dot_general(a, b, (((1,), (1,)), ((), ())), preferred_element_type=F32)


def _split_bf16(x):
    hi = x.astype(BF16)
    lo = (x - hi.astype(F32)).astype(BF16)
    return hi, lo


def _rms(x, g):
    ms = jnp.mean(x * x, axis=-1, keepdims=True)
    return x * lax.rsqrt(ms + EPS) * g


def _sigmoid(x):
    return 1.0 / (1.0 + jnp.exp(-x))


def _silu(x):
    return x * _sigmoid(x)


def _gelu_tanh(x):
    c = math.sqrt(2.0 / math.pi)
    return 0.5 * x * (1.0 + jnp.tanh(c * (x + 0.044715 * (x * x * x))))


def _ssm_prep_kernel(are, aim, ldt, bre, bim, oar, oai, obr, obi):
    dt = jnp.exp(ldt[...])
    lr = are[...]
    li = aim[...]
    mag = jnp.exp(lr * dt)
    ar = mag * jnp.cos(li * dt)
    ai = mag * jnp.sin(li * dt)
    xr = ar - 1.0
    den = lr * lr + li * li
    qr = (xr * lr + ai * li) / den
    qi = (ai * lr - xr * li) / den
    oar[...] = ar
    oai[...] = ai
    obr[...] = qr * bre[...] - qi * bim[...]
    obi[...] = qr * bim[...] + qi * bre[...]


def _ssm_prep(a_re, a_im, log_dt, b_re, b_im):
    n_dir = a_re.shape[0]
    rows = n_dir * SSM_GROUPS * SSM_STATE
    full = (n_dir, SSM_GROUPS, SSM_STATE, SSM_GROUP)
    bc = lambda t: jnp.broadcast_to(t, full).reshape(rows, SSM_GROUP).astype(F32)
    args = (bc(a_re[..., None]), bc(a_im[..., None]), bc(log_dt[..., None, None]),
            b_re.reshape(rows, SSM_GROUP).astype(F32), b_im.reshape(rows, SSM_GROUP).astype(F32))
    shp = jax.ShapeDtypeStruct((rows, SSM_GROUP), F32)
    oar, oai, obr, obi = pl.pallas_call(
        _ssm_prep_kernel, out_shape=(shp, shp, shp, shp), name="ssm_prep")(*args)
    rs = lambda t: t.reshape(full)
    return rs(oar)[..., 0], rs(oai)[..., 0], rs(obr), rs(obi)


def _proj_kernel(sc_ref, x_ref, g_ref, w_ref, qg_ref, kg_ref, cos_ref, sin_ref, ones_ref,
                 q0_ref, q1_ref, k_ref, v_ref, u_ref, ga_ref, gs_ref):
    tm = x_ref.shape[0]
    xn = _rms(x_ref[...], g_ref[...]).astype(BF16)
    proj = _dot(xn, w_ref[...])
    cos = cos_ref[...]
    sin = sin_ref[...]
    ones = ones_ref[...]
    lane = lax.broadcasted_iota(jnp.int32, (tm, LANES), 1)
    first = (lane % ROT_HALF) < ROT_QUARTER

    def norm_rope(t, gain):
        hi, lo = _split_bf16(t * t)
        ss = _dot(hi, ones) + _dot(lo, ones)
        tn = t * lax.rsqrt(ss * (1.0 / HEAD_DIM) + EPS) * gain
        partner = jnp.where(first, pltpu.roll(tn, LANES - ROT_QUARTER, 1),
                            pltpu.roll(tn, ROT_QUARTER, 1))
        return tn * cos + partner * sin

    scale = Q_SCALE * sc_ref[0]
    group0 = lane < HEAD_DIM
    for j in range(Q_W // LANES):
        sl = slice(j * LANES, (j + 1) * LANES)
        qj = norm_rope(proj[:, sl], qg_ref[...]) * scale
        q0_ref[:, sl] = jnp.where(group0, qj, 0.0).astype(F8)
        q1_ref[:, sl] = jnp.where(group0, 0.0, qj).astype(F8)
    o = Q_W
    k_ref[...] = (norm_rope(proj[:, o:o + KV_W], kg_ref[...]) * sc_ref[1]).astype(F8)
    o += KV_W
    v_ref[:, :KV_W] = (proj[:, o:o + KV_W] * sc_ref[2]).astype(F8)
    v_ref[:, KV_W:] = jnp.ones((tm, KV_W), F32).astype(F8)
    o += KV_W
    u_ref[...] = proj[:, o:o + SSM_WIDTH].astype(BF16)
    o += SSM_WIDTH
    d = ga_ref.shape[1]
    ga_ref[...] = _sigmoid(proj[:, o:o + d]).astype(BF16)
    o += d
    gs_ref[...] = _sigmoid(proj[:, o:o + d]).astype(BF16)


def _proj(scales, x2, g_mix, w_cat, qg, kg, cos_t, sin_t, ones_bd, seq_len):
    n, d = x2.shape
    tm = PROJ_TM
    per_seq = seq_len // tm
    row = lambda w: pl.BlockSpec((tm, w), lambda i: (i, 0))
    pos = pl.BlockSpec((tm, LANES), lambda i: (i % per_seq, 0))
    out = lambda w, dt=BF16: jax.ShapeDtypeStruct((n, w), dt)
    return pl.pallas_call(
        _proj_kernel,
        grid=(n // tm,),
        in_specs=[pl.BlockSpec(memory_space=pltpu.SMEM),
                  row(d), _const_spec(g_mix.shape), _const_spec(w_cat.shape),
                  _const_spec(qg.shape), _const_spec(kg.shape), pos, pos,
                  _const_spec(ones_bd.shape)],
        out_specs=[row(Q_W), row(Q_W), row(KV_W), row(2 * KV_W), row(SSM_WIDTH), row(d), row(d)],
        out_shape=[out(Q_W, F8), out(Q_W, F8), out(KV_W, F8), out(2 * KV_W, F8),
                   out(SSM_WIDTH), out(d), out(d)],
        compiler_params=_cparams("parallel"),
        name="proj",
    )(scales, x2, g_mix, w_cat, qg, kg, cos_t, sin_t, ones_bd)


P_SHIFT = 8.0


def _attn_kernel(inv_ref, q0_ref, q1_ref, k_ref, v_ref, o_ref):
    tq = q0_ref.shape[1]
    k = k_ref[0]
    v = v_ref[0]
    n_col = Q_W // LANES
    group0 = lax.broadcasted_iota(jnp.int32, (tq, LANES), 1) < HEAD_DIM

    def scores(q_ref):
        qs = jnp.concatenate([q_ref[0, :, j * LANES:(j + 1) * LANES] for j in range(n_col)], axis=0)
        return _dot_nt(qs, k)

    def values(s):
        x = (s - jnp.max(s, axis=1, keepdims=True)).astype(BF16)
        p = jnp.exp2(x * inv_ref[0].astype(BF16) + P_SHIFT).astype(F8)
        r = _dot(p, v)
        return r[:, :KV_W] * (inv_ref[1] / r[:, KV_W:KV_W + 1])

    s = [scores(q_ref) for q_ref in (q0_ref, q1_ref)]
    o = [values(sg) for sg in s]
    for j in range(n_col):
        rows = slice(j * tq, (j + 1) * tq)
        o_ref[0, :, j * LANES:(j + 1) * LANES] = jnp.where(group0, o[0][rows], o[1][rows]).astype(BF16)


def _attention(inv, q0, q1, k, v):
    b, l, _ = q0.shape
    tq = ATTN_TQ
    qspec = pl.BlockSpec((1, tq, Q_W), lambda i, j: (i, j, 0))
    return pl.pallas_call(
        _attn_kernel,
        grid=(b, l // tq),
        in_specs=[pl.BlockSpec(memory_space=pltpu.SMEM), qspec, qspec,
                  pl.BlockSpec((1, l, KV_W), lambda i, j: (i, 0, 0)),
                  pl.BlockSpec((1, l, 2 * KV_W), lambda i, j: (i, 0, 0))],
        out_specs=qspec,
        out_shape=jax.ShapeDtypeStruct((b, l, Q_W), BF16),
        compiler_params=_cparams("parallel", "parallel"),
        name="attention",
    )(inv, q0, q1, k, v)


def _ssm_kernel(uf_ref, ub_ref, bmat_ref, cmat_ref, a_ref, yf_ref, yb_ref, buf, carry):
    t_len = uf_ref.shape[1]
    c = pl.program_id(1)

    @pl.when(c == 0)
    def _():
        carry[...] = jnp.zeros_like(carry)

    nt = SSM_LANES // LANES
    for s in range(SSM_B):
        rows = slice(s * SSM_S, s * SSM_S + t_len)
        for d, u_ref in enumerate((uf_ref, ub_ref)):
            bu = _dot(u_ref[s], bmat_ref[d])
            for c in range(2 * nt):
                buf[d * 2 * nt + c, rows, :] = bu[:, c * LANES:(c + 1) * LANES]

    for d in range(2):
        coef = [(a_ref[2 * d, :, c * LANES:(c + 1) * LANES],
                 a_ref[2 * d + 1, :, c * LANES:(c + 1) * LANES]) for c in range(nt)]

        def step(i, st, d=d, coef=coef):
            t = i if d == 0 else t_len - 1 - i
            rows = pl.ds(t, SSM_B, stride=SSM_S)
            new = []
            for c in range(nt):
                pr, pi = d * 2 * nt + c, d * 2 * nt + nt + c
                ar, ai = coef[c]
                sr, si = st[2 * c], st[2 * c + 1]
                nr = ar * sr - ai * si + buf[pr, rows, :]
                ni = ar * si + ai * sr + buf[pi, rows, :]
                buf[pr, rows, :] = nr
                buf[pi, rows, :] = ni
                new += [nr, ni]
            return tuple(new)

        base = d * 2 * nt
        st = lax.fori_loop(0, t_len, step, tuple(carry[base + n] for n in range(2 * nt)))
        for n in range(2 * nt):
            carry[base + n] = st[n]

    for s in range(SSM_B):
        rows = slice(s * SSM_S, s * SSM_S + t_len)
        for d, y_ref in enumerate((yf_ref, yb_ref)):
            xs = jnp.concatenate([buf[d * 2 * nt + c, rows, :] for c in range(2 * nt)],
                                 axis=1).astype(BF16)
            y_ref[s] = _dot(xs, cmat_ref[d]).astype(y_ref.dtype)


def _ssm_scan(u, bmat, cmat, a_vec):
    b, l, w = u.shape
    t = SSM_T
    nc = l // t
    blk = (SSM_B, t, w)
    planes = 4 * SSM_LANES // LANES
    out = jax.ShapeDtypeStruct((b, l, w), BF16)
    return pl.pallas_call(
        _ssm_kernel,
        grid=(b // SSM_B, nc),
        in_specs=[pl.BlockSpec(blk, lambda i, c: (i, c, 0)),
                  pl.BlockSpec(blk, lambda i, c: (i, nc - 1 - c, 0)),
                  _const_spec(bmat.shape), _const_spec(cmat.shape), _const_spec(a_vec.shape)],
        out_specs=[pl.BlockSpec(blk, lambda i, c: (i, c, 0)),
                   pl.BlockSpec(blk, lambda i, c: (i, nc - 1 - c, 0))],
        out_shape=[out, out],
        scratch_shapes=[pltpu.VMEM((planes, SSM_B * SSM_S, LANES), F32),
                        pltpu.VMEM((planes, SSM_B, LANES), F32)],
        compiler_params=_cparams("parallel", "arbitrary"),
        name="ssm_scan",
    )(u, u, bmat, cmat, a_vec)


def _first_index_of_max(vals, idx, n, axis):
    m = jnp.max(vals, axis=axis, keepdims=True)
    return jnp.min(jnp.where(vals == m, idx, n), axis=axis, keepdims=True)


def _route(scores, bias):
    tm = scores.shape[1]
    neg = -jnp.inf
    biased = scores + bias
    b3 = biased.reshape(N_EXPERT_GROUPS, EXPERTS_PER_GROUP, tm)
    j3 = lax.broadcasted_iota(jnp.int32, b3.shape, 1)
    m1 = jnp.max(b3, axis=1, keepdims=True)
    f1 = jnp.min(jnp.where(b3 == m1, j3, EXPERTS_PER_GROUP), axis=1, keepdims=True)
    m2 = jnp.max(jnp.where(j3 == f1, neg, b3), axis=1, keepdims=True)
    gscore = (m1 + m2)[:, 0, :]
    gi = lax.broadcasted_iota(jnp.int32, gscore.shape, 0)
    gsel = jnp.zeros(gscore.shape, F32)
    cur = gscore
    for _ in range(TOPK_GROUPS):
        pick = gi == _first_index_of_max(cur, gi, N_EXPERT_GROUPS, 0)
        gsel = jnp.where(pick, 1.0, gsel)
        cur = jnp.where(pick, neg, cur)
    emask = jnp.broadcast_to(gsel[:, None, :], b3.shape) > 0.0
    masked = jnp.where(emask, b3, -1e30).reshape(N_EXPERTS, tm)
    ei = lax.broadcasted_iota(jnp.int32, masked.shape, 0)
    top_w = jnp.zeros(masked.shape, F32)
    cur = masked
    for _ in range(TOP_K):
        pick = ei == _first_index_of_max(cur, ei, N_EXPERTS, 0)
        top_w = jnp.where(pick, scores, top_w)
        cur = jnp.where(pick, neg, cur)
    return top_w / jnp.sum(top_w, axis=0, keepdims=True) * ROUTED_SCALE


def _merge_kernel(attn_ref, yf_ref, yb_ref, u_ref, ga_ref, gs_ref, x_ref,
                  dskip_ref, wglu_ref, bglu_ref, wau_ref, wsu_ref, wout_ref, gffn_ref,
                  wrh_ref, wrl_ref, rbias_ref, wsg_ref, wsup_ref, wsd_ref,
                  h_ref, xn_ref, gates_ref):
    y = (dskip_ref[...] * u_ref[...].astype(F32) + yf_ref[...].astype(F32)
         + yb_ref[...].astype(F32))
    y = _gelu_tanh(y)
    ssm = y * _sigmoid(_dot(y.astype(BF16), wglu_ref[...]) + bglu_ref[...])
    merged = (ga_ref[...].astype(F32) * _dot(attn_ref[...], wau_ref[...])
              + gs_ref[...].astype(F32) * _dot(ssm.astype(BF16), wsu_ref[...]))
    h = x_ref[...] + _dot(merged.astype(BF16), wout_ref[...])
    xn = _rms(h, gffn_ref[...])
    xh, xl = _split_bf16(xn)
    xn_ref[...] = xh
    sh = _silu(_dot(xh, wsg_ref[...])) * _dot(xh, wsup_ref[...])
    h_ref[...] = h + _dot(sh.astype(BF16), wsd_ref[...])
    wh = wrh_ref[...]
    logits = _dot_nt(wh, xh) + _dot_nt(wh, xl) + _dot_nt(wrl_ref[...], xh)
    gates_ref[...] = _route(_sigmoid(logits), rbias_ref[...]).T


def _merge(attn, yf, yb, u, ga, gs, x2, dskip, wglu, bglu, wau, wsu, wout, gffn, wrh, wrl, rbias,
           wsg, wsup, wsd):
    n, d = x2.shape
    tm = MERGE_TM
    row = lambda w: pl.BlockSpec((tm, w), lambda i: (i, 0))
    consts = (dskip, wglu, bglu, wau, wsu, wout, gffn, wrh, wrl, rbias, wsg, wsup, wsd)
    return pl.pallas_call(
        _merge_kernel,
        grid=(n // tm,),
        in_specs=[row(Q_W), row(SSM_WIDTH), row(SSM_WIDTH), row(SSM_WIDTH), row(d), row(d), row(d)]
                 + [_const_spec(c.shape) for c in consts],
        out_specs=[row(d), row(d), row(N_EXPERTS)],
        out_shape=[jax.ShapeDtypeStruct((n, d), F32), jax.ShapeDtypeStruct((n, d), BF16),
                   jax.ShapeDtypeStruct((n, N_EXPERTS), F32)],
        compiler_params=_cparams("parallel"),
        name="merge",
    )(attn, yf, yb, u, ga, gs, x2, *consts)


def _pow2_scale(max_abs):
    return jnp.exp2(jnp.floor(jnp.log2(F8_TARGET / jnp.maximum(max_abs, F8_TINY))))


def _row_max_abs(t):
    return jnp.max(jnp.abs(t), axis=1, keepdims=True)


def _moe_kernel(inv_ref, x_ref, gates_ref, wg_ref, wu_ref, wd_ref, o_ref, acc_ref, x8_ref, sx_ref):
    j = pl.program_id(1)

    @pl.when(j == 0)
    def _():
        acc_ref[...] = jnp.zeros_like(acc_ref)
        x = x_ref[...].astype(F32)
        sx = _pow2_scale(_row_max_abs(x))
        x8_ref[...] = (x * sx).astype(F8)
        sx_ref[...] = 1.0 / sx

    x8 = x8_ref[...]
    inv_sx = sx_ref[...]
    gate_pre = _dot(x8, wg_ref[...]).astype(BF16) * (inv_sx * inv_ref[0]).astype(BF16)
    g = (gates_ref[0] * (inv_sx * inv_ref[1])).astype(BF16)
    hid = _silu(gate_pre) * _dot(x8, wu_ref[...]).astype(BF16)
    hid = jnp.concatenate(
        [hid[:, e * EXPERT_DIM:(e + 1) * EXPERT_DIM] * g[:, e:e + 1] for e in range(MOE_EC)],
        axis=1)
    sh = _pow2_scale(_row_max_abs(hid).astype(F32))
    hid8 = (hid * sh.astype(BF16)).astype(F8)
    unscale = inv_ref[2] / sh
    for c0 in range(0, wd_ref.shape[1], MOE_NB):
        cols = slice(c0, c0 + MOE_NB)
        acc_ref[:, cols] += _dot(hid8, wd_ref[:, cols]) * unscale

    @pl.when(j == pl.num_programs(1) - 1)
    def _():
        o_ref[...] = acc_ref[...].astype(o_ref.dtype)


def _to_f8(w):
    s = _pow2_scale(jnp.max(jnp.abs(w)))
    return (w * s).astype(F8), 1.0 / s


def _moe(xn, gates, w_gate, w_up, w_down):
    n, d = xn.shape
    tm = MOE_TM
    ew = MOE_EC * EXPERT_DIM
    wg, inv_g = _to_f8(w_gate)
    wu, inv_u = _to_f8(w_up)
    wd, inv_d = _to_f8(w_down)
    inv = jnp.stack([inv_g, inv_u, inv_d]).astype(F32)
    row = pl.BlockSpec((tm, d), lambda i, j: (i, 0))
    return pl.pallas_call(
        _moe_kernel,
        grid=(n // tm, N_EXPERTS // MOE_EC),
        in_specs=[pl.BlockSpec(memory_space=pltpu.SMEM),
                  row,
                  pl.BlockSpec((1, tm, MOE_EC), lambda i, j: (j, i, 0)),
                  pl.BlockSpec((d, ew), lambda i, j: (0, j)),
                  pl.BlockSpec((d, ew), lambda i, j: (0, j)),
                  pl.BlockSpec((ew, d), lambda i, j: (j, 0))],
        out_specs=row,
        out_shape=jax.ShapeDtypeStruct((n, d), BF16),
        scratch_shapes=[pltpu.VMEM((tm, d), F32), pltpu.VMEM((tm, d), F8),
                        pltpu.VMEM((tm, 1), F32)],
        compiler_params=_cparams("parallel", "arbitrary"),
        name="moe",
    )(inv, xn, gates, wg, wu, wd)


def _ple_kernel(h_ref, r_ref, p_ref, gple_ref, wpg_ref, wpp_ref, gfin_ref, o_ref, *, final):
    h = h_ref[...] + r_ref[...].astype(F32)
    gate = _sigmoid(_dot(_rms(h, gple_ref[...]).astype(BF16), wpg_ref[...]))
    h = h + gate * _dot(p_ref[...].astype(BF16), wpp_ref[...])
    o_ref[...] = _rms(h, gfin_ref[...]) if final else h


def _ple(h, routed, p2, gple, wpg, wpp, gfin, final):
    n, d = h.shape
    tm = PLE_TM
    row = lambda w: pl.BlockSpec((tm, w), lambda i: (i, 0))
    consts = (gple, wpg, wpp, gfin)
    return pl.pallas_call(
        functools.partial(_ple_kernel, final=final),
        grid=(n // tm,),
        in_specs=[row(d), row(d), row(p2.shape[1])] + [_const_spec(c.shape) for c in consts],
        out_specs=row(d),
        out_shape=jax.ShapeDtypeStruct((n, d), F32),
        compiler_params=_cparams("parallel"),
        name="ple",
    )(h, routed, p2, *consts)


def _rope_tables(seq_len):
    rows = seq_len // GRID_W
    row_ids = jnp.repeat(jnp.arange(rows, dtype=F32), GRID_W)
    col_ids = jnp.tile(jnp.arange(GRID_W, dtype=F32), rows)
    inv_freq = ROPE_THETA ** (-jnp.arange(0, ROT_HALF, 2, dtype=F32) / ROT_HALF)
    ra = row_ids[:, None] * inv_freq[None, :]
    ca = col_ids[:, None] * inv_freq[None, :]
    cos = jnp.concatenate([jnp.cos(ra), jnp.cos(ra), jnp.cos(ca), jnp.cos(ca)], axis=1)
    sin = jnp.concatenate([-jnp.sin(ra), jnp.sin(ra), -jnp.sin(ca), jnp.sin(ca)], axis=1)
    reps = LANES // HEAD_DIM
    return jnp.tile(cos, (1, reps)), jnp.tile(sin, (1, reps))


def _block_diag(t, eye):
    n_dir, g, a, b = t.shape
    return jnp.einsum('dgab,gh->dgahb', t, eye).reshape(n_dir, g * a, g * b)


def _layer(h2, p2, seq_len, final, prm, cos_t, sin_t):
    (g_mix, w_in, q_norm, k_norm, w_attn_up, a_re, a_im, log_dt, b_re, b_im, c_re, c_im,
     d_skip, w_glu, b_glu, w_ssm_up, w_out, g_ffn, w_router, router_bias, w_exp_gate,
     w_exp_up, w_exp_down, w_sh_gate, w_sh_up, w_sh_down, g_ple, w_ple_gate, w_ple_proj,
     g_final) = prm
    n, d = h2.shape
    bsz = n // seq_len
    row = lambda t: t.reshape(1, -1).astype(F32)

    rep = N_HEADS // N_KV_HEADS
    order = [g * rep + j for j in range(rep) for g in range(N_KV_HEADS)]
    qcols = jnp.concatenate([jnp.arange(HEAD_DIM) + hd * HEAD_DIM for hd in order])
    w_cat = jnp.concatenate([w_in[:, :Q_W][:, qcols], w_in[:, Q_W:]], axis=1).astype(BF16)
    reps = LANES // HEAD_DIM
    qg = jnp.tile(row(q_norm), (1, reps))
    kg = jnp.tile(row(k_norm), (1, reps))
    li = jnp.arange(LANES)
    ones_bd = (li[:, None] // HEAD_DIM == li[None, :] // HEAD_DIM).astype(BF16)

    rot = math.sqrt(2.0 * HEAD_DIM)
    w_v = w_in[:, Q_W + KV_W:Q_W + 2 * KV_W].astype(F32)
    v_bound = math.sqrt(d) * jnp.max(jnp.sqrt(jnp.sum((g_mix.astype(F32)[:, None] * w_v) ** 2, axis=0)))
    sq = _pow2_scale(rot * Q_SCALE * jnp.max(jnp.abs(q_norm)).astype(F32))
    sk = _pow2_scale(rot * jnp.max(jnp.abs(k_norm)).astype(F32))
    sv = _pow2_scale(v_bound)
    q0, q1, k, v, u, ga, gs = _proj(jnp.stack([sq, sk, sv]), h2, row(g_mix), w_cat, qg, kg,
                                    cos_t, sin_t, ones_bd, seq_len)

    seq = lambda t: t.reshape(bsz, seq_len, t.shape[-1])
    attn = _attention(jnp.stack([1.0 / (sq * sk), 1.0 / sv]), seq(q0), seq(q1), seq(k),
                      seq(v)).reshape(n, Q_W)

    abar_re, abar_im, bbar_re, bbar_im = _ssm_prep(a_re, a_im, log_dt, b_re, b_im)
    eye = jnp.eye(SSM_GROUPS, dtype=F32)
    tr = lambda t: jnp.swapaxes(t, 2, 3)
    bmat = jnp.concatenate([_block_diag(tr(bbar_re), eye), _block_diag(tr(bbar_im), eye)],
                           axis=2).astype(BF16)
    cmat = jnp.concatenate([_block_diag(tr(c_re.astype(F32)), eye),
                            _block_diag(tr(-c_im.astype(F32)), eye)], axis=1).astype(BF16)
    a_vec = jnp.stack([abar_re[0], abar_im[0], abar_re[1], abar_im[1]]).reshape(4, 1, SSM_LANES)
    a_vec = jnp.broadcast_to(a_vec, (4, SSM_B, SSM_LANES))
    yf, yb = _ssm_scan(u.reshape(bsz, seq_len, SSM_WIDTH), bmat, cmat, a_vec)

    wr_t = w_router.T.astype(F32)
    wrh, wrl = _split_bf16(wr_t)
    h1, xn, gates = _merge(
        attn, yf.reshape(n, SSM_WIDTH), yb.reshape(n, SSM_WIDTH), u, ga, gs, h2,
        row(d_skip), w_glu.astype(BF16), row(b_glu), w_attn_up[qcols, :].astype(BF16),
        w_ssm_up.astype(BF16), w_out.astype(BF16), row(g_ffn), wrh, wrl,
        router_bias.reshape(-1, 1).astype(F32),
        w_sh_gate.astype(BF16), w_sh_up.astype(BF16), w_sh_down.astype(BF16))

    cat = lambda w: jnp.swapaxes(w, 0, 1).reshape(d, N_EXPERTS * EXPERT_DIM).astype(F32)
    gates_by_step = jnp.swapaxes(gates.reshape(n, N_EXPERTS // MOE_EC, MOE_EC), 0, 1)
    routed = _moe(xn, gates_by_step, cat(w_exp_gate), cat(w_exp_up),
                  w_exp_down.reshape(N_EXPERTS * EXPERT_DIM, d).astype(F32))

    return _ple(h1, routed, p2, row(g_ple), w_ple_gate.astype(BF16), w_ple_proj.astype(BF16),
                row(g_final), final)


def kernel(x, p, g_mix, w_in, q_norm, k_norm, w_attn_up, a_re, a_im, log_dt, b_re, b_im, c_re,
           c_im, d_skip, w_glu, b_glu, w_ssm_up, w_out, g_ffn, w_router, router_bias,
           w_exp_gate, w_exp_up, w_exp_down, w_sh_gate, w_sh_up, w_sh_down, g_ple,
           w_ple_gate, w_ple_proj, g_final):
    bsz, seq_len, d = x.shape
    depth = p.shape[0]
    stacked = (g_mix, w_in, q_norm, k_norm, w_attn_up, a_re, a_im, log_dt, b_re, b_im, c_re,
               c_im, d_skip, w_glu, b_glu, w_ssm_up, w_out, g_ffn, w_router, router_bias,
               w_exp_gate, w_exp_up, w_exp_down, w_sh_gate, w_sh_up, w_sh_down, g_ple,
               w_ple_gate, w_ple_proj)
    cos_t, sin_t = _rope_tables(seq_len)
    h = x.reshape(bsz * seq_len, d)
    for i in range(depth):
        prm = tuple(t[i] for t in stacked) + (g_final,)
        h = _layer(h, p[i].reshape(bsz * seq_len, -1), seq_len, i == depth - 1, prm,
                   cos_t, sin_t)
    return h.reshape(bsz, seq_len, d)
```

```python
import functools
import math

import jax
import jax.numpy as jnp
from jax import lax
from jax.experimental import pallas as pl
from jax.experimental.pallas import tpu as pltpu

F32 = jnp.float32
BF16 = jnp.bfloat16
F8 = jnp.float8_e4m3fn
F8_TARGET = 256.0
F8_TINY = 1e-30

N_HEADS = 8
N_KV_HEADS = 2
HEAD_DIM = 64
ROPE_THETA = 10000.0
GRID_W = 64
ROT_HALF = HEAD_DIM // 2
ROT_QUARTER = ROT_HALF // 2
SSM_WIDTH = 256
SSM_GROUP = 16
SSM_GROUPS = SSM_WIDTH // SSM_GROUP
SSM_STATE = 64
SSM_LANES = SSM_GROUPS * SSM_STATE
N_EXPERTS = 64
EXPERT_DIM = 128
TOP_K = 8
N_EXPERT_GROUPS = 8
TOPK_GROUPS = 4
EXPERTS_PER_GROUP = N_EXPERTS // N_EXPERT_GROUPS
ROUTED_SCALE = 2.5
EPS = 1e-6
Q_W = N_HEADS * HEAD_DIM
KV_W = N_KV_HEADS * HEAD_DIM
Q_SCALE = math.log2(math.e) / math.sqrt(HEAD_DIM)

LANES = 128
SUBLANES = 8
VMEM_LIMIT = 56 * 1024 * 1024

PROJ_TM = 512
ATTN_TQ = 256
ATTN_KC = 256
SSM_T = 128
SSM_B = SUBLANES
SSM_S = SSM_T + SUBLANES
MERGE_TM = 512
MOE_TM = 1024
MOE_EC = 16
MOE_NB = 256
PLE_TM = 1024


def _cparams(*sem):
    return pltpu.CompilerParams(dimension_semantics=sem, vmem_limit_bytes=VMEM_LIMIT)


def _const_spec(shape):
    nd = len(shape)
    return pl.BlockSpec(shape, lambda *_: (0,) * nd)


def _dot(a, b):
    return jnp.dot(a, b, preferred_element_type=F32)


def _dot_nt(a, b):
    return lax.dot_general(a, b, (((1,), (1,)), ((), ())), preferred_element_type=F32)


def _split_bf16(x):
    hi = x.astype(BF16)
    lo = (x - hi.astype(F32)).astype(BF16)
    return hi, lo


def _rms(x, g):
    ms = jnp.mean(x * x, axis=-1, keepdims=True)
    return x * lax.rsqrt(ms + EPS) * g


def _sigmoid(x):
    return 1.0 / (1.0 + jnp.exp(-x))


def _silu(x):
    return x * _sigmoid(x)


def _gelu_tanh(x):
    c = math.sqrt(2.0 / math.pi)
    return 0.5 * x * (1.0 + jnp.tanh(c * (x + 0.044715 * (x * x * x))))


def _ssm_prep_kernel(are, aim, ldt, bre, bim, oar, oai, obr, obi):
    dt = jnp.exp(ldt[...])
    lr = are[...]
    li = aim[...]
    mag = jnp.exp(lr * dt)
    ar = mag * jnp.cos(li * dt)
    ai = mag * jnp.sin(li * dt)
    xr = ar - 1.0
    den = lr * lr + li * li
    qr = (xr * lr + ai * li) / den
    qi = (ai * lr - xr * li) / den
    oar[...] = ar
    oai[...] = ai
    obr[...] = qr * bre[...] - qi * bim[...]
    obi[...] = qr * bim[...] + qi * bre[...]


def _ssm_prep(a_re, a_im, log_dt, b_re, b_im):
    n_dir = a_re.shape[0]
    rows = n_dir * SSM_GROUPS * SSM_STATE
    full = (n_dir, SSM_GROUPS, SSM_STATE, SSM_GROUP)
    bc = lambda t: jnp.broadcast_to(t, full).reshape(rows, SSM_GROUP).astype(F32)
    args = (bc(a_re[..., None]), bc(a_im[..., None]), bc(log_dt[..., None, None]),
            b_re.reshape(rows, SSM_GROUP).astype(F32), b_im.reshape(rows, SSM_GROUP).astype(F32))
    shp = jax.ShapeDtypeStruct((rows, SSM_GROUP), F32)
    oar, oai, obr, obi = pl.pallas_call(
        _ssm_prep_kernel, out_shape=(shp, shp, shp, shp), name="ssm_prep")(*args)
    rs = lambda t: t.reshape(full)
    return rs(oar)[..., 0], rs(oai)[..., 0], rs(obr), rs(obi)


def _proj_kernel(sc_ref, x_ref, g_ref, w_ref, wgate_ref, qg_ref, kg_ref, cos_ref, sin_ref, ones_ref,
                 q0_ref, q1_ref, k_ref, v_ref, u_ref, ga_ref, gs_ref):
    tm = x_ref.shape[0]
    xn_f = _rms(x_ref[...], g_ref[...])
    proj = _dot(xn_f.astype(BF16), w_ref[...])
    sx = _pow2_scale(_row_max_abs(xn_f))
    gates_pre = _dot((xn_f * sx).astype(F8), wgate_ref[...]) * (sc_ref[3] / sx)
    cos = cos_ref[...]
    sin = sin_ref[...]
    ones = ones_ref[...]
    lane = lax.broadcasted_iota(jnp.int32, (tm, LANES), 1)
    first = (lane % ROT_HALF) < ROT_QUARTER

    def norm_rope(t, gain):
        hi, lo = _split_bf16(t * t)
        ss = _dot(hi, ones) + _dot(lo, ones)
        tn = t * lax.rsqrt(ss * (1.0 / HEAD_DIM) + EPS) * gain
        partner = jnp.where(first, pltpu.roll(tn, LANES - ROT_QUARTER, 1),
                            pltpu.roll(tn, ROT_QUARTER, 1))
        return tn * cos + partner * sin

    scale = Q_SCALE * sc_ref[0]
    group0 = lane < HEAD_DIM
    for j in range(Q_W // LANES):
        sl = slice(j * LANES, (j + 1) * LANES)
        qj = norm_rope(proj[:, sl], qg_ref[...]) * scale
        q0_ref[:, sl] = jnp.where(group0, qj, 0.0).astype(F8)
        q1_ref[:, sl] = jnp.where(group0, 0.0, qj).astype(F8)
    o = Q_W
    k_ref[...] = (norm_rope(proj[:, o:o + KV_W], kg_ref[...]) * sc_ref[1]).astype(F8)
    o += KV_W
    v_ref[:, :KV_W] = (proj[:, o:o + KV_W] * sc_ref[2]).astype(F8)
    v_ref[:, KV_W:] = jnp.ones((tm, KV_W), F32).astype(F8)
    o += KV_W
    u_ref[...] = proj[:, o:o + SSM_WIDTH].astype(BF16)
    d = ga_ref.shape[1]
    ga_ref[...] = _sigmoid(gates_pre[:, :d]).astype(BF16)
    gs_ref[...] = _sigmoid(gates_pre[:, d:]).astype(BF16)


def _proj(scales, x2, g_mix, w_cat, w_gates, qg, kg, cos_t, sin_t, ones_bd, seq_len):
    n, d = x2.shape
    tm = PROJ_TM
    per_seq = seq_len // tm
    row = lambda w: pl.BlockSpec((tm, w), lambda i: (i, 0))
    pos = pl.BlockSpec((tm, LANES), lambda i: (i % per_seq, 0))
    out = lambda w, dt=BF16: jax.ShapeDtypeStruct((n, w), dt)
    return pl.pallas_call(
        _proj_kernel,
        grid=(n // tm,),
        in_specs=[pl.BlockSpec(memory_space=pltpu.SMEM),
                  row(d), _const_spec(g_mix.shape), _const_spec(w_cat.shape),
                  _const_spec(w_gates.shape),
                  _const_spec(qg.shape), _const_spec(kg.shape), pos, pos,
                  _const_spec(ones_bd.shape)],
        out_specs=[row(Q_W), row(Q_W), row(KV_W), row(2 * KV_W), row(SSM_WIDTH), row(d), row(d)],
        out_shape=[out(Q_W, F8), out(Q_W, F8), out(KV_W, F8), out(2 * KV_W, F8),
                   out(SSM_WIDTH), out(d), out(d)],
        compiler_params=_cparams("parallel"),
        name="proj",
    )(scales, x2, g_mix, w_cat, w_gates, qg, kg, cos_t, sin_t, ones_bd)


P_SHIFT = 8.0


def _attn_kernel(inv_ref, q0_ref, q1_ref, k_ref, v_ref, o_ref):
    tq = q0_ref.shape[1]
    k = k_ref[0]
    v = v_ref[0]
    n_col = Q_W // LANES
    group0 = lax.broadcasted_iota(jnp.int32, (tq, LANES), 1) < HEAD_DIM

    def scores(q_ref):
        qs = jnp.concatenate([q_ref[0, :, j * LANES:(j + 1) * LANES] for j in range(n_col)], axis=0)
        return _dot_nt(qs, k)

    def values(s):
        x = (s - jnp.max(s, axis=1, keepdims=True)).astype(BF16)
        p = jnp.exp2(x * inv_ref[0].astype(BF16) + P_SHIFT).astype(F8)
        r = _dot(p, v)
        return r[:, :KV_W] * (inv_ref[1] / r[:, KV_W:KV_W + 1])

    s = [scores(q_ref) for q_ref in (q0_ref, q1_ref)]
    o = [values(sg) for sg in s]
    for j in range(n_col):
        rows = slice(j * tq, (j + 1) * tq)
        o_ref[0, :, j * LANES:(j + 1) * LANES] = jnp.where(group0, o[0][rows], o[1][rows]).astype(BF16)


def _attention(inv, q0, q1, k, v):
    b, l, _ = q0.shape
    tq = ATTN_TQ
    qspec = pl.BlockSpec((1, tq, Q_W), lambda i, j: (i, j, 0))
    return pl.pallas_call(
        _attn_kernel,
        grid=(b, l // tq),
        in_specs=[pl.BlockSpec(memory_space=pltpu.SMEM), qspec, qspec,
                  pl.BlockSpec((1, l, KV_W), lambda i, j: (i, 0, 0)),
                  pl.BlockSpec((1, l, 2 * KV_W), lambda i, j: (i, 0, 0))],
        out_specs=qspec,
        out_shape=jax.ShapeDtypeStruct((b, l, Q_W), BF16),
        compiler_params=_cparams("parallel", "parallel"),
        name="attention",
    )(inv, q0, q1, k, v)


def _ssm_kernel(uf_ref, ub_ref, bmat_ref, cmat_ref, a_ref, yf_ref, yb_ref, buf, carry):
    t_len = uf_ref.shape[1]
    c = pl.program_id(1)

    @pl.when(c == 0)
    def _():
        carry[...] = jnp.zeros_like(carry)

    nt = SSM_LANES // LANES
    for s in range(SSM_B):
        rows = slice(s * SSM_S, s * SSM_S + t_len)
        for d, u_ref in enumerate((uf_ref, ub_ref)):
            bu = _dot(u_ref[s], bmat_ref[d])
            for c in range(2 * nt):
                buf[d * 2 * nt + c, rows, :] = bu[:, c * LANES:(c + 1) * LANES]

    for d in range(2):
        coef = [(a_ref[2 * d, :, c * LANES:(c + 1) * LANES],
                 a_ref[2 * d + 1, :, c * LANES:(c + 1) * LANES]) for c in range(nt)]

        def step(i, st, d=d, coef=coef):
            t = i if d == 0 else t_len - 1 - i
            rows = pl.ds(t, SSM_B, stride=SSM_S)
            new = []
            for c in range(nt):
                pr, pi = d * 2 * nt + c, d * 2 * nt + nt + c
                ar, ai = coef[c]
                sr, si = st[2 * c], st[2 * c + 1]
                nr = ar * sr - ai * si + buf[pr, rows, :]
                ni = ar * si + ai * sr + buf[pi, rows, :]
                buf[pr, rows, :] = nr
                buf[pi, rows, :] = ni
                new += [nr, ni]
            return tuple(new)

        base = d * 2 * nt
        st = lax.fori_loop(0, t_len, step, tuple(carry[base + n] for n in range(2 * nt)))
        for n in range(2 * nt):
            carry[base + n] = st[n]

    for s in range(SSM_B):
        rows = slice(s * SSM_S, s * SSM_S + t_len)
        for d, y_ref in enumerate((yf_ref, yb_ref)):
            xs = jnp.concatenate([buf[d * 2 * nt + c, rows, :] for c in range(2 * nt)],
                                 axis=1).astype(BF16)
            y_ref[s] = _dot(xs, cmat_ref[d]).astype(y_ref.dtype)


def _ssm_scan(u, bmat, cmat, a_vec):
    b, l, w = u.shape
    t = SSM_T
    nc = l // t
    blk = (SSM_B, t, w)
    planes = 4 * SSM_LANES // LANES
    out = jax.ShapeDtypeStruct((b, l, w), BF16)
    return pl.pallas_call(
        _ssm_kernel,
        grid=(b // SSM_B, nc),
        in_specs=[pl.BlockSpec(blk, lambda i, c: (i, c, 0)),
                  pl.BlockSpec(blk, lambda i, c: (i, nc - 1 - c, 0)),
                  _const_spec(bmat.shape), _const_spec(cmat.shape), _const_spec(a_vec.shape)],
        out_specs=[pl.BlockSpec(blk, lambda i, c: (i, c, 0)),
                   pl.BlockSpec(blk, lambda i, c: (i, nc - 1 - c, 0))],
        out_shape=[out, out],
        scratch_shapes=[pltpu.VMEM((planes, SSM_B * SSM_S, LANES), F32),
                        pltpu.VMEM((planes, SSM_B, LANES), F32)],
        compiler_params=_cparams("parallel", "arbitrary"),
        name="ssm_scan",
    )(u, u, bmat, cmat, a_vec)


def _first_index_of_max(vals, idx, n, axis):
    m = jnp.max(vals, axis=axis, keepdims=True)
    return jnp.min(jnp.where(vals == m, idx, n), axis=axis, keepdims=True)


def _route(scores, bias):
    tm = scores.shape[1]
    neg = -jnp.inf
    biased = scores + bias
    b3 = biased.reshape(N_EXPERT_GROUPS, EXPERTS_PER_GROUP, tm)
    j3 = lax.broadcasted_iota(jnp.int32, b3.shape, 1)
    m1 = jnp.max(b3, axis=1, keepdims=True)
    f1 = jnp.min(jnp.where(b3 == m1, j3, EXPERTS_PER_GROUP), axis=1, keepdims=True)
    m2 = jnp.max(jnp.where(j3 == f1, neg, b3), axis=1, keepdims=True)
    gscore = (m1 + m2)[:, 0, :]
    gi = lax.broadcasted_iota(jnp.int32, gscore.shape, 0)
    gsel = jnp.zeros(gscore.shape, F32)
    cur = gscore
    for _ in range(TOPK_GROUPS):
        pick = gi == _first_index_of_max(cur, gi, N_EXPERT_GROUPS, 0)
        gsel = jnp.where(pick, 1.0, gsel)
        cur = jnp.where(pick, neg, cur)
    emask = jnp.broadcast_to(gsel[:, None, :], b3.shape) > 0.0
    masked = jnp.where(emask, b3, -1e30).reshape(N_EXPERTS, tm)
    ei = lax.broadcasted_iota(jnp.int32, masked.shape, 0)
    top_w = jnp.zeros(masked.shape, F32)
    cur = masked
    for _ in range(TOP_K):
        pick = ei == _first_index_of_max(cur, ei, N_EXPERTS, 0)
        top_w = jnp.where(pick, scores, top_w)
        cur = jnp.where(pick, neg, cur)
    return top_w / jnp.sum(top_w, axis=0, keepdims=True) * ROUTED_SCALE


def _merge_kernel(attn_ref, yf_ref, yb_ref, u_ref, ga_ref, gs_ref, x_ref,
                  dskip_ref, wglu_ref, bglu_ref, wau_ref, wsu_ref, wout_ref, gffn_ref,
                  wrh_ref, wrl_ref, rbias_ref, wsg_ref, wsup_ref, wsd_ref,
                  h_ref, xn_ref, gates_ref):
    y = (dskip_ref[...] * u_ref[...].astype(F32) + yf_ref[...].astype(F32)
         + yb_ref[...].astype(F32))
    y = _gelu_tanh(y)
    ssm = y * _sigmoid(_dot(y.astype(BF16), wglu_ref[...]) + bglu_ref[...])
    merged = (ga_ref[...].astype(F32) * _dot(attn_ref[...], wau_ref[...])
              + gs_ref[...].astype(F32) * _dot(ssm.astype(BF16), wsu_ref[...]))
    h = x_ref[...] + _dot(merged.astype(BF16), wout_ref[...])
    xn = _rms(h, gffn_ref[...])
    xh, xl = _split_bf16(xn)
    xn_ref[...] = xh
    sh = _silu(_dot(xh, wsg_ref[...])) * _dot(xh, wsup_ref[...])
    h_ref[...] = h + _dot(sh.astype(BF16), wsd_ref[...])
    wh = wrh_ref[...]
    logits = _dot_nt(wh, xh) + _dot_nt(wh, xl) + _dot_nt(wrl_ref[...], xh)
    gates_ref[...] = _route(_sigmoid(logits), rbias_ref[...]).T


def _merge(attn, yf, yb, u, ga, gs, x2, dskip, wglu, bglu, wau, wsu, wout, gffn, wrh, wrl, rbias,
           wsg, wsup, wsd):
    n, d = x2.shape
    tm = MERGE_TM
    row = lambda w: pl.BlockSpec((tm, w), lambda i: (i, 0))
    consts = (dskip, wglu, bglu, wau, wsu, wout, gffn, wrh, wrl, rbias, wsg, wsup, wsd)
    return pl.pallas_call(
        _merge_kernel,
        grid=(n // tm,),
        in_specs=[row(Q_W), row(SSM_WIDTH), row(SSM_WIDTH), row(SSM_WIDTH), row(d), row(d), row(d)]
                 + [_const_spec(c.shape) for c in consts],
        out_specs=[row(d), row(d), row(N_EXPERTS)],
        out_shape=[jax.ShapeDtypeStruct((n, d), F32), jax.ShapeDtypeStruct((n, d), BF16),
                   jax.ShapeDtypeStruct((n, N_EXPERTS), F32)],
        compiler_params=_cparams("parallel"),
        name="merge",
    )(attn, yf, yb, u, ga, gs, x2, *consts)


def _pow2_scale(max_abs):
    return jnp.exp2(jnp.floor(jnp.log2(F8_TARGET / jnp.maximum(max_abs, F8_TINY))))


def _row_max_abs(t):
    return jnp.max(jnp.abs(t), axis=1, keepdims=True)


def _moe_kernel(inv_ref, x_ref, gates_ref, wgu_ref, wd_ref, o_ref, acc_ref, x8_ref, sx_ref):
    j = pl.program_id(1)

    @pl.when(j == 0)
    def _():
        acc_ref[...] = jnp.zeros_like(acc_ref)
        x = x_ref[...].astype(F32)
        sx = _pow2_scale(_row_max_abs(x))
        x8_ref[...] = (x * sx).astype(F8)
        sx_ref[...] = 1.0 / sx

    x8 = x8_ref[...]
    inv_sx = sx_ref[...]
    unscale_gu = (inv_sx * inv_ref[0]).astype(BF16)
    g = (gates_ref[0] * (inv_sx * inv_ref[0])).astype(BF16)
    gu = _dot(x8, wgu_ref[...]).astype(BF16)
    hid = []
    for e in range(MOE_EC):
        o = 2 * e * EXPERT_DIM
        gate_pre = gu[:, o:o + EXPERT_DIM] * unscale_gu
        hid.append(_silu(gate_pre) * gu[:, o + EXPERT_DIM:o + 2 * EXPERT_DIM] * g[:, e:e + 1])
    hid = jnp.concatenate(hid, axis=1)
    sh = _pow2_scale(_row_max_abs(hid).astype(F32))
    hid8 = (hid * sh.astype(BF16)).astype(F8)
    unscale = inv_ref[1] / sh
    for c0 in range(0, wd_ref.shape[1], MOE_NB):
        cols = slice(c0, c0 + MOE_NB)
        acc_ref[:, cols] += _dot(hid8, wd_ref[:, cols]) * unscale

    @pl.when(j == pl.num_programs(1) - 1)
    def _():
        o_ref[...] = acc_ref[...].astype(o_ref.dtype)


def _to_f8(w):
    s = _pow2_scale(jnp.max(jnp.abs(w)))
    return (w * s).astype(F8), 1.0 / s


def _moe(xn, gates, w_gate_up, w_down):
    n, d = xn.shape
    tm = MOE_TM
    ew = MOE_EC * EXPERT_DIM
    wgu, inv_gu = _to_f8(w_gate_up)
    wd, inv_d = _to_f8(w_down)
    inv = jnp.stack([inv_gu, inv_d]).astype(F32)
    row = pl.BlockSpec((tm, d), lambda i, j: (i, 0))
    return pl.pallas_call(
        _moe_kernel,
        grid=(n // tm, N_EXPERTS // MOE_EC),
        in_specs=[pl.BlockSpec(memory_space=pltpu.SMEM),
                  row,
                  pl.BlockSpec((1, tm, MOE_EC), lambda i, j: (j, i, 0)),
                  pl.BlockSpec((d, 2 * ew), lambda i, j: (0, j)),
                  pl.BlockSpec((ew, d), lambda i, j: (j, 0))],
        out_specs=row,
        out_shape=jax.ShapeDtypeStruct((n, d), BF16),
        scratch_shapes=[pltpu.VMEM((tm, d), F32), pltpu.VMEM((tm, d), F8),
                        pltpu.VMEM((tm, 1), F32)],
        compiler_params=_cparams("parallel", "arbitrary"),
        name="moe",
    )(inv, xn, gates, wgu, wd)


def _ple_kernel(h_ref, r_ref, p_ref, gple_ref, wpg_ref, wpp_ref, gfin_ref, o_ref, *, final):
    h = h_ref[...] + r_ref[...].astype(F32)
    gate = _sigmoid(_dot(_rms(h, gple_ref[...]).astype(BF16), wpg_ref[...]))
    h = h + gate * _dot(p_ref[...].astype(BF16), wpp_ref[...])
    o_ref[...] = _rms(h, gfin_ref[...]) if final else h


def _ple(h, routed, p2, gple, wpg, wpp, gfin, final):
    n, d = h.shape
    tm = PLE_TM
    row = lambda w: pl.BlockSpec((tm, w), lambda i: (i, 0))
    consts = (gple, wpg, wpp, gfin)
    return pl.pallas_call(
        functools.partial(_ple_kernel, final=final),
        grid=(n // tm,),
        in_specs=[row(d), row(d), row(p2.shape[1])] + [_const_spec(c.shape) for c in consts],
        out_specs=row(d),
        out_shape=jax.ShapeDtypeStruct((n, d), F32),
        compiler_params=_cparams("parallel"),
        name="ple",
    )(h, routed, p2, *consts)


def _rope_tables(seq_len):
    rows = seq_len // GRID_W
    row_ids = jnp.repeat(jnp.arange(rows, dtype=F32), GRID_W)
    col_ids = jnp.tile(jnp.arange(GRID_W, dtype=F32), rows)
    inv_freq = ROPE_THETA ** (-jnp.arange(0, ROT_HALF, 2, dtype=F32) / ROT_HALF)
    ra = row_ids[:, None] * inv_freq[None, :]
    ca = col_ids[:, None] * inv_freq[None, :]
    cos = jnp.concatenate([jnp.cos(ra), jnp.cos(ra), jnp.cos(ca), jnp.cos(ca)], axis=1)
    sin = jnp.concatenate([-jnp.sin(ra), jnp.sin(ra), -jnp.sin(ca), jnp.sin(ca)], axis=1)
    reps = LANES // HEAD_DIM
    return jnp.tile(cos, (1, reps)), jnp.tile(sin, (1, reps))


def _block_diag(t, eye):
    n_dir, g, a, b = t.shape
    return jnp.einsum('dgab,gh->dgahb', t, eye).reshape(n_dir, g * a, g * b)


def _layer(h2, p2, seq_len, final, prm, cos_t, sin_t):
    (g_mix, w_in, q_norm, k_norm, w_attn_up, a_re, a_im, log_dt, b_re, b_im, c_re, c_im,
     d_skip, w_glu, b_glu, w_ssm_up, w_out, g_ffn, w_router, router_bias, w_exp_gate,
     w_exp_up, w_exp_down, w_sh_gate, w_sh_up, w_sh_down, g_ple, w_ple_gate, w_ple_proj,
     g_final) = prm
    n, d = h2.shape
    bsz = n // seq_len
    row = lambda t: t.reshape(1, -1).astype(F32)

    rep = N_HEADS // N_KV_HEADS
    order = [g * rep + j for j in range(rep) for g in range(N_KV_HEADS)]
    qcols = jnp.concatenate([jnp.arange(HEAD_DIM) + hd * HEAD_DIM for hd in order])
    n_main = Q_W + 2 * KV_W + SSM_WIDTH
    w_cat = jnp.concatenate([w_in[:, :Q_W][:, qcols], w_in[:, Q_W:n_main]], axis=1).astype(BF16)
    w_gates, inv_wgates = _to_f8(w_in[:, n_main:].astype(F32))
    reps = LANES // HEAD_DIM
    qg = jnp.tile(row(q_norm), (1, reps))
    kg = jnp.tile(row(k_norm), (1, reps))
    li = jnp.arange(LANES)
    ones_bd = (li[:, None] // HEAD_DIM == li[None, :] // HEAD_DIM).astype(BF16)

    rot = math.sqrt(2.0 * HEAD_DIM)
    w_v = w_in[:, Q_W + KV_W:Q_W + 2 * KV_W].astype(F32)
    v_bound = math.sqrt(d) * jnp.max(jnp.sqrt(jnp.sum((g_mix.astype(F32)[:, None] * w_v) ** 2, axis=0)))
    sq = _pow2_scale(rot * Q_SCALE * jnp.max(jnp.abs(q_norm)).astype(F32))
    sk = _pow2_scale(rot * jnp.max(jnp.abs(k_norm)).astype(F32))
    sv = _pow2_scale(v_bound)
    q0, q1, k, v, u, ga, gs = _proj(jnp.stack([sq, sk, sv, inv_wgates]), h2, row(g_mix), w_cat,
                                    w_gates, qg, kg, cos_t, sin_t, ones_bd, seq_len)

    seq = lambda t: t.reshape(bsz, seq_len, t.shape[-1])
    attn = _attention(jnp.stack([1.0 / (sq * sk), 1.0 / sv]), seq(q0), seq(q1), seq(k),
                      seq(v)).reshape(n, Q_W)

    abar_re, abar_im, bbar_re, bbar_im = _ssm_prep(a_re, a_im, log_dt, b_re, b_im)
    eye = jnp.eye(SSM_GROUPS, dtype=F32)
    tr = lambda t: jnp.swapaxes(t, 2, 3)
    bmat = jnp.concatenate([_block_diag(tr(bbar_re), eye), _block_diag(tr(bbar_im), eye)],
                           axis=2).astype(BF16)
    cmat = jnp.concatenate([_block_diag(tr(c_re.astype(F32)), eye),
                            _block_diag(tr(-c_im.astype(F32)), eye)], axis=1).astype(BF16)
    a_vec = jnp.stack([abar_re[0], abar_im[0], abar_re[1], abar_im[1]]).reshape(4, 1, SSM_LANES)
    a_vec = jnp.broadcast_to(a_vec, (4, SSM_B, SSM_LANES))
    yf, yb = _ssm_scan(u.reshape(bsz, seq_len, SSM_WIDTH), bmat, cmat, a_vec)

    wr_t = w_router.T.astype(F32)
    wrh, wrl = _split_bf16(wr_t)
    h1, xn, gates = _merge(
        attn, yf.reshape(n, SSM_WIDTH), yb.reshape(n, SSM_WIDTH), u, ga, gs, h2,
        row(d_skip), w_glu.astype(BF16), row(b_glu), w_attn_up[qcols, :].astype(BF16),
        w_ssm_up.astype(BF16), w_out.astype(BF16), row(g_ffn), wrh, wrl,
        router_bias.reshape(-1, 1).astype(F32),
        w_sh_gate.astype(BF16), w_sh_up.astype(BF16), w_sh_down.astype(BF16))

    w_gate_up = jnp.swapaxes(jnp.concatenate([w_exp_gate, w_exp_up], axis=2), 0, 1)
    w_gate_up = w_gate_up.reshape(d, 2 * N_EXPERTS * EXPERT_DIM).astype(F32)
    gates_by_step = jnp.swapaxes(gates.reshape(n, N_EXPERTS // MOE_EC, MOE_EC), 0, 1)
    routed = _moe(xn, gates_by_step, w_gate_up,
                  w_exp_down.reshape(N_EXPERTS * EXPERT_DIM, d).astype(F32))

    return _ple(h1, routed, p2, row(g_ple), w_ple_gate.astype(BF16), w_ple_proj.astype(BF16),
                row(g_final), final)


def kernel(x, p, g_mix, w_in, q_norm, k_norm, w_attn_up, a_re, a_im, log_dt, b_re, b_im, c_re,
           c_im, d_skip, w_glu, b_glu, w_ssm_up, w_out, g_ffn, w_router, router_bias,
           w_exp_gate, w_exp_up, w_exp_down, w_sh_gate, w_sh_up, w_sh_down, g_ple,
           w_ple_gate, w_ple_proj, g_final):
    bsz, seq_len, d = x.shape
    depth = p.shape[0]
    stacked = (g_mix, w_in, q_norm, k_norm, w_attn_up, a_re, a_im, log_dt, b_re, b_im, c_re,
               c_im, d_skip, w_glu, b_glu, w_ssm_up, w_out, g_ffn, w_router, router_bias,
               w_exp_gate, w_exp_up, w_exp_down, w_sh_gate, w_sh_up, w_sh_down, g_ple,
               w_ple_gate, w_ple_proj)
    cos_t, sin_t = _rope_tables(seq_len)
    h = x.reshape(bsz * seq_len, d)
    for i in range(depth):
        prm = tuple(t[i] for t in stacked) + (g_final,)
        h = _layer(h, p[i].reshape(bsz * seq_len, -1), seq_len, i == depth - 1, prm,
                   cos_t, sin_t)
    return h.reshape(bsz, seq_len, d)
```

```python
import functools
import math

import jax
import jax.numpy as jnp
from jax import lax
from jax.experimental import pallas as pl
from jax.experimental.pallas import tpu as pltpu

F32 = jnp.float32
BF16 = jnp.bfloat16
F8 = jnp.float8_e4m3fn
F8_TARGET = 256.0
F8_TINY = 1e-30

N_HEADS = 8
N_KV_HEADS = 2
HEAD_DIM = 64
ROPE_THETA = 10000.0
GRID_W = 64
ROT_HALF = HEAD_DIM // 2
ROT_QUARTER = ROT_HALF // 2
SSM_WIDTH = 256
SSM_GROUP = 16
SSM_GROUPS = SSM_WIDTH // SSM_GROUP
SSM_STATE = 64
SSM_LANES = SSM_GROUPS * SSM_STATE
N_EXPERTS = 64
EXPERT_DIM = 128
TOP_K = 8
N_EXPERT_GROUPS = 8
TOPK_GROUPS = 4
EXPERTS_PER_GROUP = N_EXPERTS // N_EXPERT_GROUPS
ROUTED_SCALE = 2.5
EPS = 1e-6
Q_W = N_HEADS * HEAD_DIM
KV_W = N_KV_HEADS * HEAD_DIM
Q_SCALE = math.log2(math.e) / math.sqrt(HEAD_DIM)

LANES = 128
SUBLANES = 8
VMEM_LIMIT = 56 * 1024 * 1024

PROJ_TM = 512
ATTN_TQ = 256
ATTN_KC = 256
SSM_T = 128
SSM_B = SUBLANES
SSM_S = SSM_T + SUBLANES
MERGE_TM = 512
MOE_TM = 1024
MOE_EC = 16
MOE_NB = 256
PLE_TM = 1024


def _cparams(*sem):
    return pltpu.CompilerParams(dimension_semantics=sem, vmem_limit_bytes=VMEM_LIMIT)


def _const_spec(shape):
    nd = len(shape)
    return pl.BlockSpec(shape, lambda *_: (0,) * nd)


def _dot(a, b):
    return jnp.dot(a, b, preferred_element_type=F32)


def _dot_nt(a, b):
    return lax.dot_general(a, b, (((1,), (1,)), ((), ())), preferred_element_type=F32)


def _split_bf16(x):
    hi = x.astype(BF16)
    lo = (x - hi.astype(F32)).astype(BF16)
    return hi, lo


def _rms(x, g):
    ms = jnp.mean(x * x, axis=-1, keepdims=True)
    return x * lax.rsqrt(ms + EPS) * g


def _sigmoid(x):
    return 1.0 / (1.0 + jnp.exp(-x))


def _silu(x):
    return x * _sigmoid(x)


def _gelu_tanh(x):
    c = math.sqrt(2.0 / math.pi)
    return 0.5 * x * (1.0 + jnp.tanh(c * (x + 0.044715 * (x * x * x))))


def _ssm_prep_kernel(are, aim, ldt, bre, bim, oar, oai, obr, obi):
    dt = jnp.exp(ldt[...])
    lr = are[...]
    li = aim[...]
    mag = jnp.exp(lr * dt)
    ar = mag * jnp.cos(li * dt)
    ai = mag * jnp.sin(li * dt)
    xr = ar - 1.0
    den = lr * lr + li * li
    qr = (xr * lr + ai * li) / den
    qi = (ai * lr - xr * li) / den
    oar[...] = ar
    oai[...] = ai
    obr[...] = qr * bre[...] - qi * bim[...]
    obi[...] = qr * bim[...] + qi * bre[...]


def _ssm_prep(a_re, a_im, log_dt, b_re, b_im):
    n_dir = a_re.shape[0]
    rows = n_dir * SSM_GROUPS * SSM_STATE
    full = (n_dir, SSM_GROUPS, SSM_STATE, SSM_GROUP)
    bc = lambda t: jnp.broadcast_to(t, full).reshape(rows, SSM_GROUP).astype(F32)
    args = (bc(a_re[..., None]), bc(a_im[..., None]), bc(log_dt[..., None, None]),
            b_re.reshape(rows, SSM_GROUP).astype(F32), b_im.reshape(rows, SSM_GROUP).astype(F32))
    shp = jax.ShapeDtypeStruct((rows, SSM_GROUP), F32)
    oar, oai, obr, obi = pl.pallas_call(
        _ssm_prep_kernel, out_shape=(shp, shp, shp, shp), name="ssm_prep")(*args)
    rs = lambda t: t.reshape(full)
    return rs(oar)[..., 0], rs(oai)[..., 0], rs(obr), rs(obi)


def _proj_kernel(sc_ref, x_ref, g_ref, w_ref, wgate_ref, qg_ref, kg_ref, cos_ref, sin_ref, ones_ref,
                 q0_ref, q1_ref, k_ref, v_ref, u_ref, u8_ref, ga_ref, gs_ref):
    tm = x_ref.shape[0]
    xn_f = _rms(x_ref[...], g_ref[...])
    proj = _dot(xn_f.astype(BF16), w_ref[...])
    sx = _pow2_scale(_row_max_abs(xn_f))
    gates_pre = _dot((xn_f * sx).astype(F8), wgate_ref[...]) * (sc_ref[3] / sx)
    cos = cos_ref[...]
    sin = sin_ref[...]
    ones = ones_ref[...]
    lane = lax.broadcasted_iota(jnp.int32, (tm, LANES), 1)
    first = (lane % ROT_HALF) < ROT_QUARTER

    def norm_rope(t, gain):
        hi, lo = _split_bf16(t * t)
        ss = _dot(hi, ones) + _dot(lo, ones)
        tn = t * lax.rsqrt(ss * (1.0 / HEAD_DIM) + EPS) * gain
        partner = jnp.where(first, pltpu.roll(tn, LANES - ROT_QUARTER, 1),
                            pltpu.roll(tn, ROT_QUARTER, 1))
        return tn * cos + partner * sin

    scale = Q_SCALE * sc_ref[0]
    group0 = lane < HEAD_DIM
    for j in range(Q_W // LANES):
        sl = slice(j * LANES, (j + 1) * LANES)
        qj = norm_rope(proj[:, sl], qg_ref[...]) * scale
        q0_ref[:, sl] = jnp.where(group0, qj, 0.0).astype(F8)
        q1_ref[:, sl] = jnp.where(group0, 0.0, qj).astype(F8)
    o = Q_W
    k_ref[...] = (norm_rope(proj[:, o:o + KV_W], kg_ref[...]) * sc_ref[1]).astype(F8)
    o += KV_W
    v_ref[:, :KV_W] = (proj[:, o:o + KV_W] * sc_ref[2]).astype(F8)
    v_ref[:, KV_W:] = jnp.ones((tm, KV_W), F32).astype(F8)
    o += KV_W
    u_ref[...] = proj[:, o:o + SSM_WIDTH].astype(BF16)
    u8_ref[...] = (proj[:, o:o + SSM_WIDTH] * sc_ref[4]).astype(F8)
    d = ga_ref.shape[1]
    ga_ref[...] = _sigmoid(gates_pre[:, :d]).astype(BF16)
    gs_ref[...] = _sigmoid(gates_pre[:, d:]).astype(BF16)


def _proj(scales, x2, g_mix, w_cat, w_gates, qg, kg, cos_t, sin_t, ones_bd, seq_len):
    n, d = x2.shape
    tm = PROJ_TM
    per_seq = seq_len // tm
    row = lambda w: pl.BlockSpec((tm, w), lambda i: (i, 0))
    pos = pl.BlockSpec((tm, LANES), lambda i: (i % per_seq, 0))
    out = lambda w, dt=BF16: jax.ShapeDtypeStruct((n, w), dt)
    return pl.pallas_call(
        _proj_kernel,
        grid=(n // tm,),
        in_specs=[pl.BlockSpec(memory_space=pltpu.SMEM),
                  row(d), _const_spec(g_mix.shape), _const_spec(w_cat.shape),
                  _const_spec(w_gates.shape),
                  _const_spec(qg.shape), _const_spec(kg.shape), pos, pos,
                  _const_spec(ones_bd.shape)],
        out_specs=[row(Q_W), row(Q_W), row(KV_W), row(2 * KV_W), row(SSM_WIDTH), row(SSM_WIDTH),
                   row(d), row(d)],
        out_shape=[out(Q_W, F8), out(Q_W, F8), out(KV_W, F8), out(2 * KV_W, F8),
                   out(SSM_WIDTH), out(SSM_WIDTH, F8), out(d), out(d)],
        compiler_params=_cparams("parallel"),
        name="proj",
    )(scales, x2, g_mix, w_cat, w_gates, qg, kg, cos_t, sin_t, ones_bd)


P_SHIFT = 8.0


def _attn_kernel(inv_ref, q0_ref, q1_ref, k_ref, v_ref, o_ref):
    tq = q0_ref.shape[1]
    k = k_ref[0]
    v = v_ref[0]
    n_col = Q_W // LANES
    group0 = lax.broadcasted_iota(jnp.int32, (tq, LANES), 1) < HEAD_DIM

    def scores(q_ref):
        qs = jnp.concatenate([q_ref[0, :, j * LANES:(j + 1) * LANES] for j in range(n_col)], axis=0)
        return _dot_nt(qs, k)

    def values(s):
        x = (s - jnp.max(s, axis=1, keepdims=True)).astype(BF16)
        p = jnp.exp2(x * inv_ref[0].astype(BF16) + P_SHIFT).astype(F8)
        r = _dot(p, v)
        return r[:, :KV_W] * (inv_ref[1] / r[:, KV_W:KV_W + 1])

    s = [scores(q_ref) for q_ref in (q0_ref, q1_ref)]
    o = [values(sg) for sg in s]
    for j in range(n_col):
        rows = slice(j * tq, (j + 1) * tq)
        o_ref[0, :, j * LANES:(j + 1) * LANES] = jnp.where(group0, o[0][rows], o[1][rows]).astype(BF16)


def _attention(inv, q0, q1, k, v):
    b, l, _ = q0.shape
    tq = ATTN_TQ
    qspec = pl.BlockSpec((1, tq, Q_W), lambda i, j: (i, j, 0))
    return pl.pallas_call(
        _attn_kernel,
        grid=(b, l // tq),
        in_specs=[pl.BlockSpec(memory_space=pltpu.SMEM), qspec, qspec,
                  pl.BlockSpec((1, l, KV_W), lambda i, j: (i, 0, 0)),
                  pl.BlockSpec((1, l, 2 * KV_W), lambda i, j: (i, 0, 0))],
        out_specs=qspec,
        out_shape=jax.ShapeDtypeStruct((b, l, Q_W), BF16),
        compiler_params=_cparams("parallel", "parallel"),
        name="attention",
    )(inv, q0, q1, k, v)


def _ssm_kernel(uf_ref, ub_ref, bmat_ref, cmat_ref, a_ref, yf_ref, yb_ref, buf, carry):
    t_len = uf_ref.shape[1]
    c = pl.program_id(1)

    @pl.when(c == 0)
    def _():
        carry[...] = jnp.zeros_like(carry)

    nt = SSM_LANES // LANES
    for s in range(SSM_B):
        rows = slice(s * SSM_S, s * SSM_S + t_len)
        for d, u_ref in enumerate((uf_ref, ub_ref)):
            bu = _dot(u_ref[s], bmat_ref[d])
            for c in range(2 * nt):
                buf[d * 2 * nt + c, rows, :] = bu[:, c * LANES:(c + 1) * LANES]

    for d in range(2):
        coef = [(a_ref[2 * d, :, c * LANES:(c + 1) * LANES],
                 a_ref[2 * d + 1, :, c * LANES:(c + 1) * LANES]) for c in range(nt)]

        def step(i, st, d=d, coef=coef):
            t = i if d == 0 else t_len - 1 - i
            rows = pl.ds(t, SSM_B, stride=SSM_S)
            new = []
            for c in range(nt):
                pr, pi = d * 2 * nt + c, d * 2 * nt + nt + c
                ar, ai = coef[c]
                sr, si = st[2 * c], st[2 * c + 1]
                nr = ar * sr - ai * si + buf[pr, rows, :]
                ni = ar * si + ai * sr + buf[pi, rows, :]
                buf[pr, rows, :] = nr
                buf[pi, rows, :] = ni
                new += [nr, ni]
            return tuple(new)

        base = d * 2 * nt
        st = lax.fori_loop(0, t_len, step, tuple(carry[base + n] for n in range(2 * nt)))
        for n in range(2 * nt):
            carry[base + n] = st[n]

    for s in range(SSM_B):
        rows = slice(s * SSM_S, s * SSM_S + t_len)
        for d, y_ref in enumerate((yf_ref, yb_ref)):
            xs = jnp.concatenate([buf[d * 2 * nt + c, rows, :] for c in range(2 * nt)],
                                 axis=1).astype(BF16)
            y_ref[s] = _dot(xs, cmat_ref[d]).astype(y_ref.dtype)


def _ssm_scan(u, bmat, cmat, a_vec):
    b, l, w = u.shape
    t = SSM_T
    nc = l // t
    blk = (SSM_B, t, w)
    planes = 4 * SSM_LANES // LANES
    out = jax.ShapeDtypeStruct((b, l, w), BF16)
    return pl.pallas_call(
        _ssm_kernel,
        grid=(b // SSM_B, nc),
        in_specs=[pl.BlockSpec(blk, lambda i, c: (i, c, 0)),
                  pl.BlockSpec(blk, lambda i, c: (i, nc - 1 - c, 0)),
                  _const_spec(bmat.shape), _const_spec(cmat.shape), _const_spec(a_vec.shape)],
        out_specs=[pl.BlockSpec(blk, lambda i, c: (i, c, 0)),
                   pl.BlockSpec(blk, lambda i, c: (i, nc - 1 - c, 0))],
        out_shape=[out, out],
        scratch_shapes=[pltpu.VMEM((planes, SSM_B * SSM_S, LANES), F32),
                        pltpu.VMEM((planes, SSM_B, LANES), F32)],
        compiler_params=_cparams("parallel", "arbitrary"),
        name="ssm_scan",
    )(u, u, bmat, cmat, a_vec)


def _first_index_of_max(vals, idx, n, axis):
    m = jnp.max(vals, axis=axis, keepdims=True)
    return jnp.min(jnp.where(vals == m, idx, n), axis=axis, keepdims=True)


def _route(scores, bias):
    tm = scores.shape[1]
    neg = -jnp.inf
    biased = scores + bias
    b3 = biased.reshape(N_EXPERT_GROUPS, EXPERTS_PER_GROUP, tm)
    j3 = lax.broadcasted_iota(jnp.int32, b3.shape, 1)
    m1 = jnp.max(b3, axis=1, keepdims=True)
    f1 = jnp.min(jnp.where(b3 == m1, j3, EXPERTS_PER_GROUP), axis=1, keepdims=True)
    m2 = jnp.max(jnp.where(j3 == f1, neg, b3), axis=1, keepdims=True)
    gscore = (m1 + m2)[:, 0, :]
    gi = lax.broadcasted_iota(jnp.int32, gscore.shape, 0)
    gsel = jnp.zeros(gscore.shape, F32)
    cur = gscore
    for _ in range(TOPK_GROUPS):
        pick = gi == _first_index_of_max(cur, gi, N_EXPERT_GROUPS, 0)
        gsel = jnp.where(pick, 1.0, gsel)
        cur = jnp.where(pick, neg, cur)
    emask = jnp.broadcast_to(gsel[:, None, :], b3.shape) > 0.0
    masked = jnp.where(emask, b3, -1e30).reshape(N_EXPERTS, tm)
    ei = lax.broadcasted_iota(jnp.int32, masked.shape, 0)
    top_w = jnp.zeros(masked.shape, F32)
    cur = masked
    for _ in range(TOP_K):
        pick = ei == _first_index_of_max(cur, ei, N_EXPERTS, 0)
        top_w = jnp.where(pick, scores, top_w)
        cur = jnp.where(pick, neg, cur)
    return top_w / jnp.sum(top_w, axis=0, keepdims=True) * ROUTED_SCALE


def _merge_kernel(attn_ref, yf_ref, yb_ref, u_ref, ga_ref, gs_ref, x_ref,
                  dskip_ref, wglu_ref, bglu_ref, wau_ref, wsu_ref, wout_ref, gffn_ref,
                  wrh_ref, wrl_ref, rbias_ref, wsg_ref, wsup_ref, wsd_ref,
                  h_ref, xn8_ref, isx_ref, gates_ref):
    y = (dskip_ref[...] * u_ref[...].astype(F32) + yf_ref[...].astype(F32)
         + yb_ref[...].astype(F32))
    y = _gelu_tanh(y)
    ssm = y * _sigmoid(_dot(y.astype(BF16), wglu_ref[...]) + bglu_ref[...])
    merged = (ga_ref[...].astype(F32) * _dot(attn_ref[...], wau_ref[...])
              + gs_ref[...].astype(F32) * _dot(ssm.astype(BF16), wsu_ref[...]))
    h = x_ref[...] + _dot(merged.astype(BF16), wout_ref[...])
    xn = _rms(h, gffn_ref[...])
    xh, xl = _split_bf16(xn)
    sx = _pow2_scale(_row_max_abs(xn))
    xn8_ref[...] = (xn * sx).astype(F8)
    isx_ref[...] = 1.0 / sx
    sh = _silu(_dot(xh, wsg_ref[...])) * _dot(xh, wsup_ref[...])
    h_ref[...] = h + _dot(sh.astype(BF16), wsd_ref[...])
    wh = wrh_ref[...]
    logits = _dot_nt(wh, xh) + _dot_nt(wh, xl) + _dot_nt(wrl_ref[...], xh)
    gates_ref[...] = _route(_sigmoid(logits), rbias_ref[...]).T


def _merge(attn, yf, yb, u, ga, gs, x2, dskip, wglu, bglu, wau, wsu, wout, gffn, wrh, wrl, rbias,
           wsg, wsup, wsd):
    n, d = x2.shape
    tm = MERGE_TM
    row = lambda w: pl.BlockSpec((tm, w), lambda i: (i, 0))
    consts = (dskip, wglu, bglu, wau, wsu, wout, gffn, wrh, wrl, rbias, wsg, wsup, wsd)
    return pl.pallas_call(
        _merge_kernel,
        grid=(n // tm,),
        in_specs=[row(Q_W), row(SSM_WIDTH), row(SSM_WIDTH), row(SSM_WIDTH), row(d), row(d), row(d)]
                 + [_const_spec(c.shape) for c in consts],
        out_specs=[row(d), row(d), row(1), row(N_EXPERTS)],
        out_shape=[jax.ShapeDtypeStruct((n, d), F32), jax.ShapeDtypeStruct((n, d), F8),
                   jax.ShapeDtypeStruct((n, 1), F32), jax.ShapeDtypeStruct((n, N_EXPERTS), F32)],
        compiler_params=_cparams("parallel"),
        name="merge",
    )(attn, yf, yb, u, ga, gs, x2, *consts)


def _pow2_scale(max_abs):
    return jnp.exp2(jnp.floor(jnp.log2(F8_TARGET / jnp.maximum(max_abs, F8_TINY))))


def _row_max_abs(t):
    return jnp.max(jnp.abs(t), axis=1, keepdims=True)


def _moe_kernel(inv_ref, x8_ref, isx_ref, gates_ref, wg_ref, wu_ref, wd_ref, o_ref, acc_ref):
    j = pl.program_id(1)

    @pl.when(j == 0)
    def _():
        acc_ref[...] = jnp.zeros_like(acc_ref)

    x8 = x8_ref[...]
    inv_sx = isx_ref[...]
    gate_pre = _dot(x8, wg_ref[...]).astype(BF16) * (inv_sx * inv_ref[0]).astype(BF16)
    g = (gates_ref[0] * (inv_sx * inv_ref[1])).astype(BF16)
    hid = _silu(gate_pre) * _dot(x8, wu_ref[...]).astype(BF16)
    hid = jnp.concatenate(
        [hid[:, e * EXPERT_DIM:(e + 1) * EXPERT_DIM] * g[:, e:e + 1] for e in range(MOE_EC)],
        axis=1)
    sh = _pow2_scale(_row_max_abs(hid).astype(F32))
    hid8 = (hid * sh.astype(BF16)).astype(F8)
    unscale = inv_ref[2] / sh
    for c0 in range(0, wd_ref.shape[1], MOE_NB):
        cols = slice(c0, c0 + MOE_NB)
        acc_ref[:, cols] += _dot(hid8, wd_ref[:, cols]) * unscale

    @pl.when(j == pl.num_programs(1) - 1)
    def _():
        o_ref[...] = acc_ref[...].astype(o_ref.dtype)


def _to_f8(w):
    s = _pow2_scale(jnp.max(jnp.abs(w)))
    return (w * s).astype(F8), 1.0 / s


def _moe(xn8, inv_sx, gates, w_gate, w_up, w_down):
    n, d = xn8.shape
    tm = MOE_TM
    ew = MOE_EC * EXPERT_DIM
    wg, inv_g = _to_f8(w_gate)
    wu, inv_u = _to_f8(w_up)
    wd, inv_d = _to_f8(w_down)
    inv = jnp.stack([inv_g, inv_u, inv_d]).astype(F32)
    row = pl.BlockSpec((tm, d), lambda i, j: (i, 0))
    return pl.pallas_call(
        _moe_kernel,
        grid=(n // tm, N_EXPERTS // MOE_EC),
        in_specs=[pl.BlockSpec(memory_space=pltpu.SMEM),
                  row,
                  pl.BlockSpec((tm, 1), lambda i, j: (i, 0)),
                  pl.BlockSpec((1, tm, MOE_EC), lambda i, j: (j, i, 0)),
                  pl.BlockSpec((d, ew), lambda i, j: (0, j)),
                  pl.BlockSpec((d, ew), lambda i, j: (0, j)),
                  pl.BlockSpec((ew, d), lambda i, j: (j, 0))],
        out_specs=row,
        out_shape=jax.ShapeDtypeStruct((n, d), BF16),
        scratch_shapes=[pltpu.VMEM((tm, d), F32)],
        compiler_params=_cparams("parallel", "arbitrary"),
        name="moe",
    )(inv, xn8, inv_sx, gates, wg, wu, wd)


def _ple_kernel(h_ref, r_ref, p_ref, gple_ref, wpg_ref, wpp_ref, gfin_ref, o_ref, *, final):
    h = h_ref[...] + r_ref[...].astype(F32)
    gate = _sigmoid(_dot(_rms(h, gple_ref[...]).astype(BF16), wpg_ref[...]))
    h = h + gate * _dot(p_ref[...].astype(BF16), wpp_ref[...])
    o_ref[...] = _rms(h, gfin_ref[...]) if final else h


def _ple(h, routed, p2, gple, wpg, wpp, gfin, final):
    n, d = h.shape
    tm = PLE_TM
    row = lambda w: pl.BlockSpec((tm, w), lambda i: (i, 0))
    consts = (gple, wpg, wpp, gfin)
    return pl.pallas_call(
        functools.partial(_ple_kernel, final=final),
        grid=(n // tm,),
        in_specs=[row(d), row(d), row(p2.shape[1])] + [_const_spec(c.shape) for c in consts],
        out_specs=row(d),
        out_shape=jax.ShapeDtypeStruct((n, d), F32),
        compiler_params=_cparams("parallel"),
        name="ple",
    )(h, routed, p2, *consts)


def _rope_tables(seq_len):
    rows = seq_len // GRID_W
    row_ids = jnp.repeat(jnp.arange(rows, dtype=F32), GRID_W)
    col_ids = jnp.tile(jnp.arange(GRID_W, dtype=F32), rows)
    inv_freq = ROPE_THETA ** (-jnp.arange(0, ROT_HALF, 2, dtype=F32) / ROT_HALF)
    ra = row_ids[:, None] * inv_freq[None, :]
    ca = col_ids[:, None] * inv_freq[None, :]
    cos = jnp.concatenate([jnp.cos(ra), jnp.cos(ra), jnp.cos(ca), jnp.cos(ca)], axis=1)
    sin = jnp.concatenate([-jnp.sin(ra), jnp.sin(ra), -jnp.sin(ca), jnp.sin(ca)], axis=1)
    reps = LANES // HEAD_DIM
    return jnp.tile(cos, (1, reps)), jnp.tile(sin, (1, reps))


def _block_diag(t, eye):
    n_dir, g, a, b = t.shape
    return jnp.einsum('dgab,gh->dgahb', t, eye).reshape(n_dir, g * a, g * b)


def _layer(h2, p2, seq_len, final, prm, cos_t, sin_t):
    (g_mix, w_in, q_norm, k_norm, w_attn_up, a_re, a_im, log_dt, b_re, b_im, c_re, c_im,
     d_skip, w_glu, b_glu, w_ssm_up, w_out, g_ffn, w_router, router_bias, w_exp_gate,
     w_exp_up, w_exp_down, w_sh_gate, w_sh_up, w_sh_down, g_ple, w_ple_gate, w_ple_proj,
     g_final) = prm
    n, d = h2.shape
    bsz = n // seq_len
    row = lambda t: t.reshape(1, -1).astype(F32)

    rep = N_HEADS // N_KV_HEADS
    order = [g * rep + j for j in range(rep) for g in range(N_KV_HEADS)]
    qcols = jnp.concatenate([jnp.arange(HEAD_DIM) + hd * HEAD_DIM for hd in order])
    n_main = Q_W + 2 * KV_W + SSM_WIDTH
    w_cat = jnp.concatenate([w_in[:, :Q_W][:, qcols], w_in[:, Q_W:n_main]], axis=1).astype(BF16)
    w_gates, inv_wgates = _to_f8(w_in[:, n_main:].astype(F32))
    reps = LANES // HEAD_DIM
    qg = jnp.tile(row(q_norm), (1, reps))
    kg = jnp.tile(row(k_norm), (1, reps))
    li = jnp.arange(LANES)
    ones_bd = (li[:, None] // HEAD_DIM == li[None, :] // HEAD_DIM).astype(BF16)

    rot = math.sqrt(2.0 * HEAD_DIM)
    w_v = w_in[:, Q_W + KV_W:Q_W + 2 * KV_W].astype(F32)
    v_bound = math.sqrt(d) * jnp.max(jnp.sqrt(jnp.sum((g_mix.astype(F32)[:, None] * w_v) ** 2, axis=0)))
    sq = _pow2_scale(rot * Q_SCALE * jnp.max(jnp.abs(q_norm)).astype(F32))
    sk = _pow2_scale(rot * jnp.max(jnp.abs(k_norm)).astype(F32))
    sv = _pow2_scale(v_bound)
    w_u = w_in[:, Q_W + 2 * KV_W:n_main].astype(F32)
    su = _pow2_scale(math.sqrt(d) * jnp.max(jnp.sqrt(jnp.sum((g_mix.astype(F32)[:, None] * w_u) ** 2,
                                                             axis=0))))
    q0, q1, k, v, u, u8, ga, gs = _proj(jnp.stack([sq, sk, sv, inv_wgates, su]), h2, row(g_mix),
                                        w_cat, w_gates, qg, kg, cos_t, sin_t, ones_bd, seq_len)

    seq = lambda t: t.reshape(bsz, seq_len, t.shape[-1])
    attn = _attention(jnp.stack([1.0 / (sq * sk), 1.0 / sv]), seq(q0), seq(q1), seq(k),
                      seq(v)).reshape(n, Q_W)

    abar_re, abar_im, bbar_re, bbar_im = _ssm_prep(a_re, a_im, log_dt, b_re, b_im)
    eye = jnp.eye(SSM_GROUPS, dtype=F32)
    tr = lambda t: jnp.swapaxes(t, 2, 3)
    bmat, inv_sb = _to_f8(jnp.concatenate([_block_diag(tr(bbar_re), eye),
                                           _block_diag(tr(bbar_im), eye)], axis=2))
    cmat = jnp.concatenate([_block_diag(tr(c_re.astype(F32)), eye),
                            _block_diag(tr(-c_im.astype(F32)), eye)], axis=1)
    cmat = (cmat * (inv_sb / su)).astype(BF16)
    a_vec = jnp.stack([abar_re[0], abar_im[0], abar_re[1], abar_im[1]]).reshape(4, 1, SSM_LANES)
    a_vec = jnp.broadcast_to(a_vec, (4, SSM_B, SSM_LANES))
    yf, yb = _ssm_scan(u8.reshape(bsz, seq_len, SSM_WIDTH), bmat, cmat, a_vec)

    wr_t = w_router.T.astype(F32)
    wrh, wrl = _split_bf16(wr_t)
    h1, xn8, inv_sx, gates = _merge(
        attn, yf.reshape(n, SSM_WIDTH), yb.reshape(n, SSM_WIDTH), u, ga, gs, h2,
        row(d_skip), w_glu.astype(BF16), row(b_glu), w_attn_up[qcols, :].astype(BF16),
        w_ssm_up.astype(BF16), w_out.astype(BF16), row(g_ffn), wrh, wrl,
        router_bias.reshape(-1, 1).astype(F32),
        w_sh_gate.astype(BF16), w_sh_up.astype(BF16), w_sh_down.astype(BF16))

    cat = lambda w: jnp.swapaxes(w, 0, 1).reshape(d, N_EXPERTS * EXPERT_DIM).astype(F32)
    gates_by_step = jnp.swapaxes(gates.reshape(n, N_EXPERTS // MOE_EC, MOE_EC), 0, 1)
    routed = _moe(xn8, inv_sx, gates_by_step, cat(w_exp_gate), cat(w_exp_up),
                  w_exp_down.reshape(N_EXPERTS * EXPERT_DIM, d).astype(F32))

    return _ple(h1, routed, p2, row(g_ple), w_ple_gate.astype(BF16), w_ple_proj.astype(BF16),
                row(g_final), final)


def kernel(x, p, g_mix, w_in, q_norm, k_norm, w_attn_up, a_re, a_im, log_dt, b_re, b_im, c_re,
           c_im, d_skip, w_glu, b_glu, w_ssm_up, w_out, g_ffn, w_router, router_bias,
           w_exp_gate, w_exp_up, w_exp_down, w_sh_gate, w_sh_up, w_sh_down, g_ple,
           w_ple_gate, w_ple_proj, g_final):
    bsz, seq_len, d = x.shape
    depth = p.shape[0]
    stacked = (g_mix, w_in, q_norm, k_norm, w_attn_up, a_re, a_im, log_dt, b_re, b_im, c_re,
               c_im, d_skip, w_glu, b_glu, w_ssm_up, w_out, g_ffn, w_router, router_bias,
               w_exp_gate, w_exp_up, w_exp_down, w_sh_gate, w_sh_up, w_sh_down, g_ple,
               w_ple_gate, w_ple_proj)
    cos_t, sin_t = _rope_tables(seq_len)
    h = x.reshape(bsz * seq_len, d)
    for i in range(depth):
        prm = tuple(t[i] for t in stacked) + (g_final,)
        h = _layer(h, p[i].reshape(bsz * seq_len, -1), seq_len, i == depth - 1, prm,
                   cos_t, sin_t)
    return h.reshape(bsz, seq_len, d)
```

```python
import functools
import math

import jax
import jax.numpy as jnp
from jax import lax
from jax.experimental import pallas as pl
from jax.experimental.pallas import tpu as pltpu

F32 = jnp.float32
BF16 = jnp.bfloat16
F8 = jnp.float8_e4m3fn
F8_TARGET = 256.0
F8_TINY = 1e-30

N_HEADS = 8
N_KV_HEADS = 2
HEAD_DIM = 64
ROPE_THETA = 10000.0
GRID_W = 64
ROT_HALF = HEAD_DIM // 2
ROT_QUARTER = ROT_HALF // 2
SSM_WIDTH = 256
SSM_GROUP = 16
SSM_GROUPS = SSM_WIDTH // SSM_GROUP
SSM_STATE = 64
SSM_LANES = SSM_GROUPS * SSM_STATE
N_EXPERTS = 64
EXPERT_DIM = 128
TOP_K = 8
N_EXPERT_GROUPS = 8
TOPK_GROUPS = 4
EXPERTS_PER_GROUP = N_EXPERTS // N_EXPERT_GROUPS
ROUTED_SCALE = 2.5
EPS = 1e-6
Q_W = N_HEADS * HEAD_DIM
KV_W = N_KV_HEADS * HEAD_DIM
Q_SCALE = math.log2(math.e) / math.sqrt(HEAD_DIM)

LANES = 128
SUBLANES = 8
VMEM_LIMIT = 56 * 1024 * 1024

PROJ_TM = 512
ATTN_TQ = 256
ATTN_KC = 256
SSM_T = 128
SSM_B = SUBLANES
SSM_S = SSM_T + SUBLANES
MERGE_TM = 512
MOE_TM = 1024
MOE_EC = 16
MOE_NB = 256
PLE_TM = 1024


def _cparams(*sem):
    return pltpu.CompilerParams(dimension_semantics=sem, vmem_limit_bytes=VMEM_LIMIT)


def _const_spec(shape):
    nd = len(shape)
    return pl.BlockSpec(shape, lambda *_: (0,) * nd)


def _dot(a, b):
    return jnp.dot(a, b, preferred_element_type=F32)


def _dot_nt(a, b):
    return lax.dot_general(a, b, (((1,), (1,)), ((), ())), preferred_element_type=F32)


def _split_bf16(x):
    hi = x.astype(BF16)
    lo = (x - hi.astype(F32)).astype(BF16)
    return hi, lo


def _rms(x, g):
    ms = jnp.mean(x * x, axis=-1, keepdims=True)
    return x * lax.rsqrt(ms + EPS) * g


def _sigmoid(x):
    return 1.0 / (1.0 + jnp.exp(-x))


def _silu(x):
    return x * _sigmoid(x)


def _gelu_tanh(x):
    c = math.sqrt(2.0 / math.pi)
    return 0.5 * x * (1.0 + jnp.tanh(c * (x + 0.044715 * (x * x * x))))


def _ssm_prep_kernel(are, aim, ldt, bre, bim, oar, oai, obr, obi):
    dt = jnp.exp(ldt[...])
    lr = are[...]
    li = aim[...]
    mag = jnp.exp(lr * dt)
    ar = mag * jnp.cos(li * dt)
    ai = mag * jnp.sin(li * dt)
    xr = ar - 1.0
    den = lr * lr + li * li
    qr = (xr * lr + ai * li) / den
    qi = (ai * lr - xr * li) / den
    oar[...] = ar
    oai[...] = ai
    obr[...] = qr * bre[...] - qi * bim[...]
    obi[...] = qr * bim[...] + qi * bre[...]


def _ssm_prep(a_re, a_im, log_dt, b_re, b_im):
    n_dir = a_re.shape[0]
    rows = n_dir * SSM_GROUPS * SSM_STATE
    full = (n_dir, SSM_GROUPS, SSM_STATE, SSM_GROUP)
    bc = lambda t: jnp.broadcast_to(t, full).reshape(rows, SSM_GROUP).astype(F32)
    args = (bc(a_re[..., None]), bc(a_im[..., None]), bc(log_dt[..., None, None]),
            b_re.reshape(rows, SSM_GROUP).astype(F32), b_im.reshape(rows, SSM_GROUP).astype(F32))
    shp = jax.ShapeDtypeStruct((rows, SSM_GROUP), F32)
    oar, oai, obr, obi = pl.pallas_call(
        _ssm_prep_kernel, out_shape=(shp, shp, shp, shp), name="ssm_prep")(*args)
    rs = lambda t: t.reshape(full)
    return rs(oar)[..., 0], rs(oai)[..., 0], rs(obr), rs(obi)


def _proj_kernel(sc_ref, x_ref, g_ref, w_ref, wgate_ref, qg_ref, kg_ref, cos_ref, sin_ref, ones_ref,
                 q0_ref, q1_ref, k_ref, v_ref, u_ref, u8_ref, ga_ref, gs_ref):
    tm = x_ref.shape[0]
    xn_f = _rms(x_ref[...], g_ref[...])
    proj = _dot(xn_f.astype(BF16), w_ref[...])
    sx = _pow2_scale(_row_max_abs(xn_f))
    gates_pre = _dot((xn_f * sx).astype(F8), wgate_ref[...]) * (sc_ref[3] / sx)
    cos = cos_ref[...]
    sin = sin_ref[...]
    ones = ones_ref[...]
    lane = lax.broadcasted_iota(jnp.int32, (tm, LANES), 1)
    first = (lane % ROT_HALF) < ROT_QUARTER

    def norm_rope(t, gain):
        hi, lo = _split_bf16(t * t)
        ss = _dot(hi, ones) + _dot(lo, ones)
        tn = t * lax.rsqrt(ss * (1.0 / HEAD_DIM) + EPS) * gain
        partner = jnp.where(first, pltpu.roll(tn, LANES - ROT_QUARTER, 1),
                            pltpu.roll(tn, ROT_QUARTER, 1))
        return tn * cos + partner * sin

    scale = Q_SCALE * sc_ref[0]
    group0 = lane < HEAD_DIM
    for j in range(Q_W // LANES):
        sl = slice(j * LANES, (j + 1) * LANES)
        qj = norm_rope(proj[:, sl], qg_ref[...]) * scale
        q0_ref[:, sl] = jnp.where(group0, qj, 0.0).astype(F8)
        q1_ref[:, sl] = jnp.where(group0, 0.0, qj).astype(F8)
    o = Q_W
    k_ref[...] = (norm_rope(proj[:, o:o + KV_W], kg_ref[...]) * sc_ref[1]).astype(F8)
    o += KV_W
    v_ref[:, :KV_W] = (proj[:, o:o + KV_W] * sc_ref[2]).astype(F8)
    v_ref[:, KV_W:] = jnp.ones((tm, KV_W), F32).astype(F8)
    o += KV_W
    u_ref[...] = proj[:, o:o + SSM_WIDTH].astype(BF16)
    u8_ref[...] = (proj[:, o:o + SSM_WIDTH] * sc_ref[4]).astype(F8)
    d = ga_ref.shape[1]
    ga_ref[...] = _sigmoid(gates_pre[:, :d]).astype(BF16)
    gs_ref[...] = _sigmoid(gates_pre[:, d:]).astype(BF16)


def _proj(scales, x2, g_mix, w_cat, w_gates, qg, kg, cos_t, sin_t, ones_bd, seq_len):
    n, d = x2.shape
    tm = PROJ_TM
    per_seq = seq_len // tm
    row = lambda w: pl.BlockSpec((tm, w), lambda i: (i, 0))
    pos = pl.BlockSpec((tm, LANES), lambda i: (i % per_seq, 0))
    out = lambda w, dt=BF16: jax.ShapeDtypeStruct((n, w), dt)
    return pl.pallas_call(
        _proj_kernel,
        grid=(n // tm,),
        in_specs=[pl.BlockSpec(memory_space=pltpu.SMEM),
                  row(d), _const_spec(g_mix.shape), _const_spec(w_cat.shape),
                  _const_spec(w_gates.shape),
                  _const_spec(qg.shape), _const_spec(kg.shape), pos, pos,
                  _const_spec(ones_bd.shape)],
        out_specs=[row(Q_W), row(Q_W), row(KV_W), row(2 * KV_W), row(SSM_WIDTH), row(SSM_WIDTH),
                   row(d), row(d)],
        out_shape=[out(Q_W, F8), out(Q_W, F8), out(KV_W, F8), out(2 * KV_W, F8),
                   out(SSM_WIDTH), out(SSM_WIDTH, F8), out(d), out(d)],
        compiler_params=_cparams("parallel"),
        name="proj",
    )(scales, x2, g_mix, w_cat, w_gates, qg, kg, cos_t, sin_t, ones_bd)


P_SHIFT = 8.0


def _attn_kernel(inv_ref, q0_ref, q1_ref, k_ref, v_ref, o_ref):
    tq = q0_ref.shape[1]
    k = k_ref[0]
    v = v_ref[0]
    n_col = Q_W // LANES
    group0 = lax.broadcasted_iota(jnp.int32, (tq, LANES), 1) < HEAD_DIM

    def scores(q_ref):
        qs = jnp.concatenate([q_ref[0, :, j * LANES:(j + 1) * LANES] for j in range(n_col)], axis=0)
        return _dot_nt(qs, k)

    def values(s):
        sb = s.astype(BF16)
        x = sb - jnp.max(sb, axis=1, keepdims=True)
        p = jnp.exp2(x * inv_ref[0].astype(BF16) + P_SHIFT).astype(F8)
        r = _dot(p, v)
        return r[:, :KV_W] * (inv_ref[1] / r[:, KV_W:KV_W + 1])

    s = [scores(q_ref) for q_ref in (q0_ref, q1_ref)]
    o = [values(sg) for sg in s]
    for j in range(n_col):
        rows = slice(j * tq, (j + 1) * tq)
        o_ref[0, :, j * LANES:(j + 1) * LANES] = jnp.where(group0, o[0][rows], o[1][rows]).astype(BF16)


def _attention(inv, q0, q1, k, v):
    b, l, _ = q0.shape
    tq = ATTN_TQ
    qspec = pl.BlockSpec((1, tq, Q_W), lambda i, j: (i, j, 0))
    return pl.pallas_call(
        _attn_kernel,
        grid=(b, l // tq),
        in_specs=[pl.BlockSpec(memory_space=pltpu.SMEM), qspec, qspec,
                  pl.BlockSpec((1, l, KV_W), lambda i, j: (i, 0, 0)),
                  pl.BlockSpec((1, l, 2 * KV_W), lambda i, j: (i, 0, 0))],
        out_specs=qspec,
        out_shape=jax.ShapeDtypeStruct((b, l, Q_W), BF16),
        compiler_params=_cparams("parallel", "parallel"),
        name="attention",
    )(inv, q0, q1, k, v)


def _ssm_kernel(uf_ref, ub_ref, bmat_ref, cmat_ref, a_ref, yf_ref, yb_ref, buf, carry):
    t_len = uf_ref.shape[1]
    c = pl.program_id(1)

    @pl.when(c == 0)
    def _():
        carry[...] = jnp.zeros_like(carry)

    nt = SSM_LANES // LANES
    for s in range(SSM_B):
        rows = slice(s * SSM_S, s * SSM_S + t_len)
        for d, u_ref in enumerate((uf_ref, ub_ref)):
            bu = _dot(u_ref[s], bmat_ref[d])
            for c in range(2 * nt):
                buf[d * 2 * nt + c, rows, :] = bu[:, c * LANES:(c + 1) * LANES]

    for d in range(2):
        coef = [(a_ref[2 * d, :, c * LANES:(c + 1) * LANES],
                 a_ref[2 * d + 1, :, c * LANES:(c + 1) * LANES]) for c in range(nt)]

        def step(i, st, d=d, coef=coef):
            t = i if d == 0 else t_len - 1 - i
            rows = pl.ds(t, SSM_B, stride=SSM_S)
            new = []
            for c in range(nt):
                pr, pi = d * 2 * nt + c, d * 2 * nt + nt + c
                ar, ai = coef[c]
                sr, si = st[2 * c], st[2 * c + 1]
                nr = ar * sr - ai * si + buf[pr, rows, :]
                ni = ar * si + ai * sr + buf[pi, rows, :]
                buf[pr, rows, :] = nr
                buf[pi, rows, :] = ni
                new += [nr, ni]
            return tuple(new)

        base = d * 2 * nt
        st = lax.fori_loop(0, t_len, step, tuple(carry[base + n] for n in range(2 * nt)))
        for n in range(2 * nt):
            carry[base + n] = st[n]

    for s in range(SSM_B):
        rows = slice(s * SSM_S, s * SSM_S + t_len)
        for d, y_ref in enumerate((yf_ref, yb_ref)):
            xs = jnp.concatenate([buf[d * 2 * nt + c, rows, :] for c in range(2 * nt)],
                                 axis=1).astype(BF16)
            y_ref[s] = _dot(xs, cmat_ref[d]).astype(y_ref.dtype)


def _ssm_scan(u, bmat, cmat, a_vec):
    b, l, w = u.shape
    t = SSM_T
    nc = l // t
    blk = (SSM_B, t, w)
    planes = 4 * SSM_LANES // LANES
    out = jax.ShapeDtypeStruct((b, l, w), BF16)
    return pl.pallas_call(
        _ssm_kernel,
        grid=(b // SSM_B, nc),
        in_specs=[pl.BlockSpec(blk, lambda i, c: (i, c, 0)),
                  pl.BlockSpec(blk, lambda i, c: (i, nc - 1 - c, 0)),
                  _const_spec(bmat.shape), _const_spec(cmat.shape), _const_spec(a_vec.shape)],
        out_specs=[pl.BlockSpec(blk, lambda i, c: (i, c, 0)),
                   pl.BlockSpec(blk, lambda i, c: (i, nc - 1 - c, 0))],
        out_shape=[out, out],
        scratch_shapes=[pltpu.VMEM((planes, SSM_B * SSM_S, LANES), F32),
                        pltpu.VMEM((planes, SSM_B, LANES), F32)],
        compiler_params=_cparams("parallel", "arbitrary"),
        name="ssm_scan",
    )(u, u, bmat, cmat, a_vec)


def _first_index_of_max(vals, idx, n, axis):
    m = jnp.max(vals, axis=axis, keepdims=True)
    return jnp.min(jnp.where(vals == m, idx, n), axis=axis, keepdims=True)


def _route(scores, bias):
    tm = scores.shape[1]
    neg = -jnp.inf
    biased = scores + bias
    b3 = biased.reshape(N_EXPERT_GROUPS, EXPERTS_PER_GROUP, tm)
    j3 = lax.broadcasted_iota(jnp.int32, b3.shape, 1)
    m1 = jnp.max(b3, axis=1, keepdims=True)
    f1 = jnp.min(jnp.where(b3 == m1, j3, EXPERTS_PER_GROUP), axis=1, keepdims=True)
    m2 = jnp.max(jnp.where(j3 == f1, neg, b3), axis=1, keepdims=True)
    gscore = (m1 + m2)[:, 0, :]
    gi = lax.broadcasted_iota(jnp.int32, gscore.shape, 0)
    gsel = jnp.zeros(gscore.shape, F32)
    cur = gscore
    for _ in range(TOPK_GROUPS):
        pick = gi == _first_index_of_max(cur, gi, N_EXPERT_GROUPS, 0)
        gsel = jnp.where(pick, 1.0, gsel)
        cur = jnp.where(pick, neg, cur)
    emask = jnp.broadcast_to(gsel[:, None, :], b3.shape) > 0.0
    masked = jnp.where(emask, b3, -1e30).reshape(N_EXPERTS, tm)
    ei = lax.broadcasted_iota(jnp.int32, masked.shape, 0)
    top_w = jnp.zeros(masked.shape, F32)
    cur = masked
    for _ in range(TOP_K):
        pick = ei == _first_index_of_max(cur, ei, N_EXPERTS, 0)
        top_w = jnp.where(pick, scores, top_w)
        cur = jnp.where(pick, neg, cur)
    return top_w / jnp.sum(top_w, axis=0, keepdims=True) * ROUTED_SCALE


def _merge_kernel(attn_ref, yf_ref, yb_ref, u_ref, ga_ref, gs_ref, x_ref,
                  dskip_ref, wglu_ref, bglu_ref, wau_ref, wsu_ref, wout_ref, gffn_ref,
                  wrh_ref, wrl_ref, rbias_ref, wsg_ref, wsup_ref, wsd_ref,
                  h_ref, xn8_ref, isx_ref, gates_ref):
    y = (dskip_ref[...] * u_ref[...].astype(F32) + yf_ref[...].astype(F32)
         + yb_ref[...].astype(F32))
    y = _gelu_tanh(y)
    ssm = y * _sigmoid(_dot(y.astype(BF16), wglu_ref[...]) + bglu_ref[...])
    merged = (ga_ref[...] * _dot(attn_ref[...], wau_ref[...]).astype(BF16)
              + gs_ref[...] * _dot(ssm.astype(BF16), wsu_ref[...]).astype(BF16))
    h = x_ref[...] + _dot(merged, wout_ref[...])
    xn = _rms(h, gffn_ref[...])
    xh, xl = _split_bf16(xn)
    sx = _pow2_scale(_row_max_abs(xn))
    xn8_ref[...] = (xn * sx).astype(F8)
    isx_ref[...] = 1.0 / sx
    sh = _silu(_dot(xh, wsg_ref[...])) * _dot(xh, wsup_ref[...])
    h_ref[...] = h + _dot(sh.astype(BF16), wsd_ref[...])
    wh = wrh_ref[...]
    logits = _dot_nt(wh, xh) + _dot_nt(wh, xl) + _dot_nt(wrl_ref[...], xh)
    gates_ref[...] = _route(_sigmoid(logits), rbias_ref[...]).T


def _merge(attn, yf, yb, u, ga, gs, x2, dskip, wglu, bglu, wau, wsu, wout, gffn, wrh, wrl, rbias,
           wsg, wsup, wsd):
    n, d = x2.shape
    tm = MERGE_TM
    row = lambda w: pl.BlockSpec((tm, w), lambda i: (i, 0))
    consts = (dskip, wglu, bglu, wau, wsu, wout, gffn, wrh, wrl, rbias, wsg, wsup, wsd)
    return pl.pallas_call(
        _merge_kernel,
        grid=(n // tm,),
        in_specs=[row(Q_W), row(SSM_WIDTH), row(SSM_WIDTH), row(SSM_WIDTH), row(d), row(d), row(d)]
                 + [_const_spec(c.shape) for c in consts],
        out_specs=[row(d), row(d), row(1), row(N_EXPERTS)],
        out_shape=[jax.ShapeDtypeStruct((n, d), F32), jax.ShapeDtypeStruct((n, d), F8),
                   jax.ShapeDtypeStruct((n, 1), F32), jax.ShapeDtypeStruct((n, N_EXPERTS), F32)],
        compiler_params=_cparams("parallel"),
        name="merge",
    )(attn, yf, yb, u, ga, gs, x2, *consts)


def _pow2_scale(max_abs):
    return jnp.exp2(jnp.floor(jnp.log2(F8_TARGET / jnp.maximum(max_abs, F8_TINY))))


def _row_max_abs(t):
    return jnp.max(jnp.abs(t), axis=1, keepdims=True)


def _moe_kernel(inv_ref, x8_ref, isx_ref, gates_ref, wg_ref, wu_ref, wd_ref, o_ref, acc_ref):
    j = pl.program_id(1)

    @pl.when(j == 0)
    def _():
        acc_ref[...] = jnp.zeros_like(acc_ref)

    x8 = x8_ref[...]
    inv_sx = isx_ref[...]
    gate_pre = _dot(x8, wg_ref[...]).astype(BF16) * (inv_sx * inv_ref[0]).astype(BF16)
    g = (gates_ref[0] * (inv_sx * inv_ref[1])).astype(BF16)
    hid = _silu(gate_pre) * _dot(x8, wu_ref[...]).astype(BF16)
    hid = jnp.concatenate(
        [hid[:, e * EXPERT_DIM:(e + 1) * EXPERT_DIM] * g[:, e:e + 1] for e in range(MOE_EC)],
        axis=1)
    sh = _pow2_scale(_row_max_abs(hid).astype(F32))
    hid8 = (hid * sh.astype(BF16)).astype(F8)
    unscale = inv_ref[2] / sh
    for c0 in range(0, wd_ref.shape[1], MOE_NB):
        cols = slice(c0, c0 + MOE_NB)
        acc_ref[:, cols] += _dot(hid8, wd_ref[:, cols]) * unscale

    @pl.when(j == pl.num_programs(1) - 1)
    def _():
        o_ref[...] = acc_ref[...].astype(o_ref.dtype)


def _to_f8(w):
    s = _pow2_scale(jnp.max(jnp.abs(w)))
    return (w * s).astype(F8), 1.0 / s


def _moe(xn8, inv_sx, gates, w_gate, w_up, w_down):
    n, d = xn8.shape
    tm = MOE_TM
    ew = MOE_EC * EXPERT_DIM
    wg, inv_g = _to_f8(w_gate)
    wu, inv_u = _to_f8(w_up)
    wd, inv_d = _to_f8(w_down)
    inv = jnp.stack([inv_g, inv_u, inv_d]).astype(F32)
    row = pl.BlockSpec((tm, d), lambda i, j: (i, 0))
    return pl.pallas_call(
        _moe_kernel,
        grid=(n // tm, N_EXPERTS // MOE_EC),
        in_specs=[pl.BlockSpec(memory_space=pltpu.SMEM),
                  row,
                  pl.BlockSpec((tm, 1), lambda i, j: (i, 0)),
                  pl.BlockSpec((1, tm, MOE_EC), lambda i, j: (j, i, 0)),
                  pl.BlockSpec((d, ew), lambda i, j: (0, j)),
                  pl.BlockSpec((d, ew), lambda i, j: (0, j)),
                  pl.BlockSpec((ew, d), lambda i, j: (j, 0))],
        out_specs=row,
        out_shape=jax.ShapeDtypeStruct((n, d), BF16),
        scratch_shapes=[pltpu.VMEM((tm, d), F32)],
        compiler_params=_cparams("parallel", "arbitrary"),
        name="moe",
    )(inv, xn8, inv_sx, gates, wg, wu, wd)


def _ple_kernel(h_ref, r_ref, p_ref, gple_ref, wpg_ref, wpp_ref, gfin_ref, o_ref, *, final):
    h = h_ref[...] + r_ref[...].astype(F32)
    gate = _sigmoid(_dot(_rms(h, gple_ref[...]).astype(BF16), wpg_ref[...]))
    h = h + gate * _dot(p_ref[...].astype(BF16), wpp_ref[...])
    o_ref[...] = _rms(h, gfin_ref[...]) if final else h


def _ple(h, routed, p2, gple, wpg, wpp, gfin, final):
    n, d = h.shape
    tm = PLE_TM
    row = lambda w: pl.BlockSpec((tm, w), lambda i: (i, 0))
    consts = (gple, wpg, wpp, gfin)
    return pl.pallas_call(
        functools.partial(_ple_kernel, final=final),
        grid=(n // tm,),
        in_specs=[row(d), row(d), row(p2.shape[1])] + [_const_spec(c.shape) for c in consts],
        out_specs=row(d),
        out_shape=jax.ShapeDtypeStruct((n, d), F32),
        compiler_params=_cparams("parallel"),
        name="ple",
    )(h, routed, p2, *consts)


def _rope_tables(seq_len):
    rows = seq_len // GRID_W
    row_ids = jnp.repeat(jnp.arange(rows, dtype=F32), GRID_W)
    col_ids = jnp.tile(jnp.arange(GRID_W, dtype=F32), rows)
    inv_freq = ROPE_THETA ** (-jnp.arange(0, ROT_HALF, 2, dtype=F32) / ROT_HALF)
    ra = row_ids[:, None] * inv_freq[None, :]
    ca = col_ids[:, None] * inv_freq[None, :]
    cos = jnp.concatenate([jnp.cos(ra), jnp.cos(ra), jnp.cos(ca), jnp.cos(ca)], axis=1)
    sin = jnp.concatenate([-jnp.sin(ra), jnp.sin(ra), -jnp.sin(ca), jnp.sin(ca)], axis=1)
    reps = LANES // HEAD_DIM
    return jnp.tile(cos, (1, reps)), jnp.tile(sin, (1, reps))


def _block_diag(t, eye):
    n_dir, g, a, b = t.shape
    return jnp.einsum('dgab,gh->dgahb', t, eye).reshape(n_dir, g * a, g * b)


def _layer(h2, p2, seq_len, final, prm, cos_t, sin_t):
    (g_mix, w_in, q_norm, k_norm, w_attn_up, a_re, a_im, log_dt, b_re, b_im, c_re, c_im,
     d_skip, w_glu, b_glu, w_ssm_up, w_out, g_ffn, w_router, router_bias, w_exp_gate,
     w_exp_up, w_exp_down, w_sh_gate, w_sh_up, w_sh_down, g_ple, w_ple_gate, w_ple_proj,
     g_final) = prm
    n, d = h2.shape
    bsz = n // seq_len
    row = lambda t: t.reshape(1, -1).astype(F32)

    rep = N_HEADS // N_KV_HEADS
    order = [g * rep + j for j in range(rep) for g in range(N_KV_HEADS)]
    qcols = jnp.concatenate([jnp.arange(HEAD_DIM) + hd * HEAD_DIM for hd in order])
    n_main = Q_W + 2 * KV_W + SSM_WIDTH
    w_cat = jnp.concatenate([w_in[:, :Q_W][:, qcols], w_in[:, Q_W:n_main]], axis=1).astype(BF16)
    w_gates, inv_wgates = _to_f8(w_in[:, n_main:].astype(F32))
    reps = LANES // HEAD_DIM
    qg = jnp.tile(row(q_norm), (1, reps))
    kg = jnp.tile(row(k_norm), (1, reps))
    li = jnp.arange(LANES)
    ones_bd = (li[:, None] // HEAD_DIM == li[None, :] // HEAD_DIM).astype(BF16)

    rot = math.sqrt(2.0 * HEAD_DIM)
    w_v = w_in[:, Q_W + KV_W:Q_W + 2 * KV_W].astype(F32)
    v_bound = math.sqrt(d) * jnp.max(jnp.sqrt(jnp.sum((g_mix.astype(F32)[:, None] * w_v) ** 2, axis=0)))
    sq = _pow2_scale(rot * Q_SCALE * jnp.max(jnp.abs(q_norm)).astype(F32))
    sk = _pow2_scale(rot * jnp.max(jnp.abs(k_norm)).astype(F32))
    sv = _pow2_scale(v_bound)
    w_u = w_in[:, Q_W + 2 * KV_W:n_main].astype(F32)
    su = _pow2_scale(math.sqrt(d) * jnp.max(jnp.sqrt(jnp.sum((g_mix.astype(F32)[:, None] * w_u) ** 2,
                                                             axis=0))))
    q0, q1, k, v, u, u8, ga, gs = _proj(jnp.stack([sq, sk, sv, inv_wgates, su]), h2, row(g_mix),
                                        w_cat, w_gates, qg, kg, cos_t, sin_t, ones_bd, seq_len)

    seq = lambda t: t.reshape(bsz, seq_len, t.shape[-1])
    attn = _attention(jnp.stack([1.0 / (sq * sk), 1.0 / sv]), seq(q0), seq(q1), seq(k),
                      seq(v)).reshape(n, Q_W)

    abar_re, abar_im, bbar_re, bbar_im = _ssm_prep(a_re, a_im, log_dt, b_re, b_im)
    eye = jnp.eye(SSM_GROUPS, dtype=F32)
    tr = lambda t: jnp.swapaxes(t, 2, 3)
    bmat, inv_sb = _to_f8(jnp.concatenate([_block_diag(tr(bbar_re), eye),
                                           _block_diag(tr(bbar_im), eye)], axis=2))
    cmat = jnp.concatenate([_block_diag(tr(c_re.astype(F32)), eye),
                            _block_diag(tr(-c_im.astype(F32)), eye)], axis=1)
    cmat = (cmat * (inv_sb / su)).astype(BF16)
    a_vec = jnp.stack([abar_re[0], abar_im[0], abar_re[1], abar_im[1]]).reshape(4, 1, SSM_LANES)
    a_vec = jnp.broadcast_to(a_vec, (4, SSM_B, SSM_LANES))
    yf, yb = _ssm_scan(u8.reshape(bsz, seq_len, SSM_WIDTH), bmat, cmat, a_vec)

    wr_t = w_router.T.astype(F32)
    wrh, wrl = _split_bf16(wr_t)
    h1, xn8, inv_sx, gates = _merge(
        attn, yf.reshape(n, SSM_WIDTH), yb.reshape(n, SSM_WIDTH), u, ga, gs, h2,
        row(d_skip), w_glu.astype(BF16), row(b_glu), w_attn_up[qcols, :].astype(BF16),
        w_ssm_up.astype(BF16), w_out.astype(BF16), row(g_ffn), wrh, wrl,
        router_bias.reshape(-1, 1).astype(F32),
        w_sh_gate.astype(BF16), w_sh_up.astype(BF16), w_sh_down.astype(BF16))

    cat = lambda w: jnp.swapaxes(w, 0, 1).reshape(d, N_EXPERTS * EXPERT_DIM).astype(F32)
    gates_by_step = jnp.swapaxes(gates.reshape(n, N_EXPERTS // MOE_EC, MOE_EC), 0, 1)
    routed = _moe(xn8, inv_sx, gates_by_step, cat(w_exp_gate), cat(w_exp_up),
                  w_exp_down.reshape(N_EXPERTS * EXPERT_DIM, d).astype(F32))

    return _ple(h1, routed, p2, row(g_ple), w_ple_gate.astype(BF16), w_ple_proj.astype(BF16),
                row(g_final), final)


def kernel(x, p, g_mix, w_in, q_norm, k_norm, w_attn_up, a_re, a_im, log_dt, b_re, b_im, c_re,
           c_im, d_skip, w_glu, b_glu, w_ssm_up, w_out, g_ffn, w_router, router_bias,
           w_exp_gate, w_exp_up, w_exp_down, w_sh_gate, w_sh_up, w_sh_down, g_ple,
           w_ple_gate, w_ple_proj, g_final):
    bsz, seq_len, d = x.shape
    depth = p.shape[0]
    stacked = (g_mix, w_in, q_norm, k_norm, w_attn_up, a_re, a_im, log_dt, b_re, b_im, c_re,
               c_im, d_skip, w_glu, b_glu, w_ssm_up, w_out, g_ffn, w_router, router_bias,
               w_exp_gate, w_exp_up, w_exp_down, w_sh_gate, w_sh_up, w_sh_down, g_ple,
               w_ple_gate, w_ple_proj)
    cos_t, sin_t = _rope_tables(seq_len)
    h = x.reshape(bsz * seq_len, d)
    for i in range(depth):
        prm = tuple(t[i] for t in stacked) + (g_final,)
        h = _layer(h, p[i].reshape(bsz * seq_len, -1), seq_len, i == depth - 1, prm,
                   cos_t, sin_t)
    return h.reshape(bsz, seq_len, d)
```

```python
import functools
import math

import jax
import jax.numpy as jnp
from jax import lax
from jax.experimental import pallas as pl
from jax.experimental.pallas import tpu as pltpu

F32 = jnp.float32
BF16 = jnp.bfloat16
F8 = jnp.float8_e4m3fn
F8_TARGET = 256.0
F8_TINY = 1e-30

N_HEADS = 8
N_KV_HEADS = 2
HEAD_DIM = 64
ROPE_THETA = 10000.0
GRID_W = 64
ROT_HALF = HEAD_DIM // 2
ROT_QUARTER = ROT_HALF // 2
SSM_WIDTH = 256
SSM_GROUP = 16
SSM_GROUPS = SSM_WIDTH // SSM_GROUP
SSM_STATE = 64
SSM_LANES = SSM_GROUPS * SSM_STATE
N_EXPERTS = 64
EXPERT_DIM = 128
TOP_K = 8
N_EXPERT_GROUPS = 8
TOPK_GROUPS = 4
EXPERTS_PER_GROUP = N_EXPERTS // N_EXPERT_GROUPS
ROUTED_SCALE = 2.5
EPS = 1e-6
Q_W = N_HEADS * HEAD_DIM
KV_W = N_KV_HEADS * HEAD_DIM
Q_SCALE = math.log2(math.e) / math.sqrt(HEAD_DIM)

LANES = 128
SUBLANES = 8
VMEM_LIMIT = 56 * 1024 * 1024

PROJ_TM = 512
ATTN_TQ = 256
ATTN_KC = 256
SSM_T = 256
SSM_B = SUBLANES
SSM_S = SSM_T + SUBLANES
MERGE_TM = 1024
MOE_TM = 1024
MOE_EC = 16
MOE_NB = 256
PLE_TM = 1024


def _cparams(*sem):
    return pltpu.CompilerParams(dimension_semantics=sem, vmem_limit_bytes=VMEM_LIMIT)


def _const_spec(shape):
    nd = len(shape)
    return pl.BlockSpec(shape, lambda *_: (0,) * nd)


def _dot(a, b):
    return jnp.dot(a, b, preferred_element_type=F32)


def _dot_nt(a, b):
    return lax.dot_general(a, b, (((1,), (1,)), ((), ())), preferred_element_type=F32)


def _split_bf16(x):
    hi = x.astype(BF16)
    lo = (x - hi.astype(F32)).astype(BF16)
    return hi, lo


def _rms(x, g):
    ms = jnp.mean(x * x, axis=-1, keepdims=True)
    return x * lax.rsqrt(ms + EPS) * g


def _sigmoid(x):
    return 1.0 / (1.0 + jnp.exp(-x))


def _silu(x):
    return x * _sigmoid(x)


def _gelu_tanh(x):
    c = math.sqrt(2.0 / math.pi)
    return 0.5 * x * (1.0 + jnp.tanh(c * (x + 0.044715 * (x * x * x))))


def _ssm_prep_kernel(are, aim, ldt, bre, bim, oar, oai, obr, obi):
    dt = jnp.exp(ldt[...])
    lr = are[...]
    li = aim[...]
    mag = jnp.exp(lr * dt)
    ar = mag * jnp.cos(li * dt)
    ai = mag * jnp.sin(li * dt)
    xr = ar - 1.0
    den = lr * lr + li * li
    qr = (xr * lr + ai * li) / den
    qi = (ai * lr - xr * li) / den
    oar[...] = ar
    oai[...] = ai
    obr[...] = qr * bre[...] - qi * bim[...]
    obi[...] = qr * bim[...] + qi * bre[...]


def _ssm_prep(a_re, a_im, log_dt, b_re, b_im):
    n_dir = a_re.shape[0]
    rows = n_dir * SSM_GROUPS * SSM_STATE
    full = (n_dir, SSM_GROUPS, SSM_STATE, SSM_GROUP)
    bc = lambda t: jnp.broadcast_to(t, full).reshape(rows, SSM_GROUP).astype(F32)
    args = (bc(a_re[..., None]), bc(a_im[..., None]), bc(log_dt[..., None, None]),
            b_re.reshape(rows, SSM_GROUP).astype(F32), b_im.reshape(rows, SSM_GROUP).astype(F32))
    shp = jax.ShapeDtypeStruct((rows, SSM_GROUP), F32)
    oar, oai, obr, obi = pl.pallas_call(
        _ssm_prep_kernel, out_shape=(shp, shp, shp, shp), name="ssm_prep")(*args)
    rs = lambda t: t.reshape(full)
    return rs(oar)[..., 0], rs(oai)[..., 0], rs(obr), rs(obi)


def _proj_kernel(sc_ref, x_ref, g_ref, w_ref, wgate_ref, qg_ref, kg_ref, cos_ref, sin_ref, ones_ref,
                 q0_ref, q1_ref, k_ref, v_ref, u_ref, u8_ref, ga_ref, gs_ref):
    tm = x_ref.shape[0]
    xn_f = _rms(x_ref[...], g_ref[...])
    proj = _dot(xn_f.astype(BF16), w_ref[...])
    sx = _pow2_scale(_row_max_abs(xn_f))
    gates_pre = _dot((xn_f * sx).astype(F8), wgate_ref[...]) * (sc_ref[3] / sx)
    cos = cos_ref[...]
    sin = sin_ref[...]
    ones = ones_ref[...]
    lane = lax.broadcasted_iota(jnp.int32, (tm, LANES), 1)
    first = (lane % ROT_HALF) < ROT_QUARTER

    def norm_rope(t, gain):
        hi, lo = _split_bf16(t * t)
        ss = _dot(hi, ones) + _dot(lo, ones)
        tn = t * lax.rsqrt(ss * (1.0 / HEAD_DIM) + EPS) * gain
        partner = jnp.where(first, pltpu.roll(tn, LANES - ROT_QUARTER, 1),
                            pltpu.roll(tn, ROT_QUARTER, 1))
        return tn * cos + partner * sin

    scale = Q_SCALE * sc_ref[0]
    group0 = lane < HEAD_DIM
    for j in range(Q_W // LANES):
        sl = slice(j * LANES, (j + 1) * LANES)
        qj = norm_rope(proj[:, sl], qg_ref[...]) * scale
        q0_ref[:, sl] = jnp.where(group0, qj, 0.0).astype(F8)
        q1_ref[:, sl] = jnp.where(group0, 0.0, qj).astype(F8)
    o = Q_W
    k_ref[...] = (norm_rope(proj[:, o:o + KV_W], kg_ref[...]) * sc_ref[1]).astype(F8)
    o += KV_W
    v_ref[:, :KV_W] = (proj[:, o:o + KV_W] * sc_ref[2]).astype(F8)
    v_ref[:, KV_W:] = jnp.ones((tm, KV_W), F32).astype(F8)
    o += KV_W
    u_ref[...] = proj[:, o:o + SSM_WIDTH].astype(BF16)
    u8_ref[...] = (proj[:, o:o + SSM_WIDTH] * sc_ref[4]).astype(F8)
    d = ga_ref.shape[1]
    ga_ref[...] = _sigmoid(gates_pre[:, :d]).astype(BF16)
    gs_ref[...] = _sigmoid(gates_pre[:, d:]).astype(BF16)


def _proj(scales, x2, g_mix, w_cat, w_gates, qg, kg, cos_t, sin_t, ones_bd, seq_len):
    n, d = x2.shape
    tm = PROJ_TM
    per_seq = seq_len // tm
    row = lambda w: pl.BlockSpec((tm, w), lambda i: (i, 0))
    pos = pl.BlockSpec((tm, LANES), lambda i: (i % per_seq, 0))
    out = lambda w, dt=BF16: jax.ShapeDtypeStruct((n, w), dt)
    return pl.pallas_call(
        _proj_kernel,
        grid=(n // tm,),
        in_specs=[pl.BlockSpec(memory_space=pltpu.SMEM),
                  row(d), _const_spec(g_mix.shape), _const_spec(w_cat.shape),
                  _const_spec(w_gates.shape),
                  _const_spec(qg.shape), _const_spec(kg.shape), pos, pos,
                  _const_spec(ones_bd.shape)],
        out_specs=[row(Q_W), row(Q_W), row(KV_W), row(2 * KV_W), row(SSM_WIDTH), row(SSM_WIDTH),
                   row(d), row(d)],
        out_shape=[out(Q_W, F8), out(Q_W, F8), out(KV_W, F8), out(2 * KV_W, F8),
                   out(SSM_WIDTH), out(SSM_WIDTH, F8), out(d), out(d)],
        compiler_params=_cparams("parallel"),
        name="proj",
    )(scales, x2, g_mix, w_cat, w_gates, qg, kg, cos_t, sin_t, ones_bd)


P_SHIFT = 8.0


def _attn_kernel(inv_ref, q0_ref, q1_ref, k_ref, v_ref, o_ref):
    tq = q0_ref.shape[1]
    k = k_ref[0]
    v = v_ref[0]
    n_col = Q_W // LANES
    group0 = lax.broadcasted_iota(jnp.int32, (tq, LANES), 1) < HEAD_DIM

    def scores(q_ref):
        qs = jnp.concatenate([q_ref[0, :, j * LANES:(j + 1) * LANES] for j in range(n_col)], axis=0)
        return _dot_nt(qs, k)

    def values(s):
        sb = s.astype(BF16)
        x = sb - jnp.max(sb, axis=1, keepdims=True)
        p = jnp.exp2(x * inv_ref[0].astype(BF16) + P_SHIFT).astype(F8)
        r = _dot(p, v)
        return r[:, :KV_W] * (inv_ref[1] / r[:, KV_W:KV_W + 1])

    s = [scores(q_ref) for q_ref in (q0_ref, q1_ref)]
    o = [values(sg) for sg in s]
    for j in range(n_col):
        rows = slice(j * tq, (j + 1) * tq)
        o_ref[0, :, j * LANES:(j + 1) * LANES] = jnp.where(group0, o[0][rows], o[1][rows]).astype(BF16)


def _attention(inv, q0, q1, k, v):
    b, l, _ = q0.shape
    tq = ATTN_TQ
    qspec = pl.BlockSpec((1, tq, Q_W), lambda i, j: (i, j, 0))
    return pl.pallas_call(
        _attn_kernel,
        grid=(b, l // tq),
        in_specs=[pl.BlockSpec(memory_space=pltpu.SMEM), qspec, qspec,
                  pl.BlockSpec((1, l, KV_W), lambda i, j: (i, 0, 0)),
                  pl.BlockSpec((1, l, 2 * KV_W), lambda i, j: (i, 0, 0))],
        out_specs=qspec,
        out_shape=jax.ShapeDtypeStruct((b, l, Q_W), BF16),
        compiler_params=_cparams("parallel", "parallel"),
        name="attention",
    )(inv, q0, q1, k, v)


def _ssm_kernel(uf_ref, ub_ref, bmat_ref, cmat_ref, a_ref, yf_ref, yb_ref, buf, carry):
    t_len = uf_ref.shape[1]
    c = pl.program_id(1)

    @pl.when(c == 0)
    def _():
        carry[...] = jnp.zeros_like(carry)

    nt = SSM_LANES // LANES
    for s in range(SSM_B):
        rows = slice(s * SSM_S, s * SSM_S + t_len)
        for d, u_ref in enumerate((uf_ref, ub_ref)):
            bu = _dot(u_ref[s], bmat_ref[d])
            for c in range(2 * nt):
                buf[d * 2 * nt + c, rows, :] = bu[:, c * LANES:(c + 1) * LANES]

    for d in range(2):
        coef = [(a_ref[2 * d, :, c * LANES:(c + 1) * LANES],
                 a_ref[2 * d + 1, :, c * LANES:(c + 1) * LANES]) for c in range(nt)]

        def step(i, st, d=d, coef=coef):
            t = i if d == 0 else t_len - 1 - i
            rows = pl.ds(t, SSM_B, stride=SSM_S)
            new = []
            for c in range(nt):
                pr, pi = d * 2 * nt + c, d * 2 * nt + nt + c
                ar, ai = coef[c]
                sr, si = st[2 * c], st[2 * c + 1]
                nr = ar * sr - ai * si + buf[pr, rows, :]
                ni = ar * si + ai * sr + buf[pi, rows, :]
                buf[pr, rows, :] = nr
                buf[pi, rows, :] = ni
                new += [nr, ni]
            return tuple(new)

        base = d * 2 * nt
        st = lax.fori_loop(0, t_len, step, tuple(carry[base + n] for n in range(2 * nt)))
        for n in range(2 * nt):
            carry[base + n] = st[n]

    for s in range(SSM_B):
        rows = slice(s * SSM_S, s * SSM_S + t_len)
        for d, y_ref in enumerate((yf_ref, yb_ref)):
            xs = jnp.concatenate([buf[d * 2 * nt + c, rows, :] for c in range(2 * nt)],
                                 axis=1).astype(BF16)
            y_ref[s] = _dot(xs, cmat_ref[d]).astype(y_ref.dtype)


def _ssm_scan(u, bmat, cmat, a_vec):
    b, l, w = u.shape
    t = SSM_T
    nc = l // t
    blk = (SSM_B, t, w)
    planes = 4 * SSM_LANES // LANES
    out = jax.ShapeDtypeStruct((b, l, w), BF16)
    return pl.pallas_call(
        _ssm_kernel,
        grid=(b // SSM_B, nc),
        in_specs=[pl.BlockSpec(blk, lambda i, c: (i, c, 0)),
                  pl.BlockSpec(blk, lambda i, c: (i, nc - 1 - c, 0)),
                  _const_spec(bmat.shape), _const_spec(cmat.shape), _const_spec(a_vec.shape)],
        out_specs=[pl.BlockSpec(blk, lambda i, c: (i, c, 0)),
                   pl.BlockSpec(blk, lambda i, c: (i, nc - 1 - c, 0))],
        out_shape=[out, out],
        scratch_shapes=[pltpu.VMEM((planes, SSM_B * SSM_S, LANES), F32),
                        pltpu.VMEM((planes, SSM_B, LANES), F32)],
        compiler_params=_cparams("parallel", "arbitrary"),
        name="ssm_scan",
    )(u, u, bmat, cmat, a_vec)


def _first_index_of_max(vals, idx, n, axis):
    m = jnp.max(vals, axis=axis, keepdims=True)
    return jnp.min(jnp.where(vals == m, idx, n), axis=axis, keepdims=True)


def _route(scores, bias):
    tm = scores.shape[1]
    neg = -jnp.inf
    biased = scores + bias
    b3 = biased.reshape(N_EXPERT_GROUPS, EXPERTS_PER_GROUP, tm)
    j3 = lax.broadcasted_iota(jnp.int32, b3.shape, 1)
    m1 = jnp.max(b3, axis=1, keepdims=True)
    f1 = jnp.min(jnp.where(b3 == m1, j3, EXPERTS_PER_GROUP), axis=1, keepdims=True)
    m2 = jnp.max(jnp.where(j3 == f1, neg, b3), axis=1, keepdims=True)
    gscore = (m1 + m2)[:, 0, :]
    gi = lax.broadcasted_iota(jnp.int32, gscore.shape, 0)
    gsel = jnp.zeros(gscore.shape, F32)
    cur = gscore
    for _ in range(TOPK_GROUPS):
        pick = gi == _first_index_of_max(cur, gi, N_EXPERT_GROUPS, 0)
        gsel = jnp.where(pick, 1.0, gsel)
        cur = jnp.where(pick, neg, cur)
    emask = jnp.broadcast_to(gsel[:, None, :], b3.shape) > 0.0
    masked = jnp.where(emask, b3, -1e30).reshape(N_EXPERTS, tm)
    ei = lax.broadcasted_iota(jnp.int32, masked.shape, 0)
    top_w = jnp.zeros(masked.shape, F32)
    cur = masked
    for _ in range(TOP_K):
        pick = ei == _first_index_of_max(cur, ei, N_EXPERTS, 0)
        top_w = jnp.where(pick, scores, top_w)
        cur = jnp.where(pick, neg, cur)
    return top_w / jnp.sum(top_w, axis=0, keepdims=True) * ROUTED_SCALE


def _merge_kernel(attn_ref, yf_ref, yb_ref, u_ref, ga_ref, gs_ref, x_ref,
                  dskip_ref, wglu_ref, bglu_ref, wau_ref, wsu_ref, wout_ref, gffn_ref,
                  wrh_ref, wrl_ref, rbias_ref, wsg_ref, wsup_ref, wsd_ref,
                  h_ref, xn8_ref, isx_ref, gates_ref):
    y = (dskip_ref[...] * u_ref[...].astype(F32) + yf_ref[...].astype(F32)
         + yb_ref[...].astype(F32))
    y = _gelu_tanh(y)
    ssm = y * _sigmoid(_dot(y.astype(BF16), wglu_ref[...]) + bglu_ref[...])
    merged = (ga_ref[...] * _dot(attn_ref[...], wau_ref[...]).astype(BF16)
              + gs_ref[...] * _dot(ssm.astype(BF16), wsu_ref[...]).astype(BF16))
    h = x_ref[...] + _dot(merged, wout_ref[...])
    xn = _rms(h, gffn_ref[...])
    xh, xl = _split_bf16(xn)
    sx = _pow2_scale(_row_max_abs(xn))
    xn8_ref[...] = (xn * sx).astype(F8)
    isx_ref[...] = 1.0 / sx
    sh = _silu(_dot(xh, wsg_ref[...])) * _dot(xh, wsup_ref[...])
    h_ref[...] = h + _dot(sh.astype(BF16), wsd_ref[...])
    wh = wrh_ref[...]
    logits = _dot_nt(wh, xh) + _dot_nt(wh, xl) + _dot_nt(wrl_ref[...], xh)
    gates_ref[...] = _route(_sigmoid(logits), rbias_ref[...]).T


def _merge(attn, yf, yb, u, ga, gs, x2, dskip, wglu, bglu, wau, wsu, wout, gffn, wrh, wrl, rbias,
           wsg, wsup, wsd):
    n, d = x2.shape
    tm = MERGE_TM
    row = lambda w: pl.BlockSpec((tm, w), lambda i: (i, 0))
    consts = (dskip, wglu, bglu, wau, wsu, wout, gffn, wrh, wrl, rbias, wsg, wsup, wsd)
    return pl.pallas_call(
        _merge_kernel,
        grid=(n // tm,),
        in_specs=[row(Q_W), row(SSM_WIDTH), row(SSM_WIDTH), row(SSM_WIDTH), row(d), row(d), row(d)]
                 + [_const_spec(c.shape) for c in consts],
        out_specs=[row(d), row(d), row(1), row(N_EXPERTS)],
        out_shape=[jax.ShapeDtypeStruct((n, d), F32), jax.ShapeDtypeStruct((n, d), F8),
                   jax.ShapeDtypeStruct((n, 1), F32), jax.ShapeDtypeStruct((n, N_EXPERTS), F32)],
        compiler_params=_cparams("parallel"),
        name="merge",
    )(attn, yf, yb, u, ga, gs, x2, *consts)


def _pow2_scale(max_abs):
    return jnp.exp2(jnp.floor(jnp.log2(F8_TARGET / jnp.maximum(max_abs, F8_TINY))))


def _row_max_abs(t):
    return jnp.max(jnp.abs(t), axis=1, keepdims=True)


def _moe_kernel(inv_ref, x8_ref, isx_ref, gates_ref, wg_ref, wu_ref, wd_ref, o_ref, acc_ref):
    j = pl.program_id(1)

    @pl.when(j == 0)
    def _():
        acc_ref[...] = jnp.zeros_like(acc_ref)

    x8 = x8_ref[...]
    inv_sx = isx_ref[...]
    gate_pre = _dot(x8, wg_ref[...]).astype(BF16) * (inv_sx * inv_ref[0]).astype(BF16)
    g = (gates_ref[0] * (inv_sx * inv_ref[1])).astype(BF16)
    hid = _silu(gate_pre) * _dot(x8, wu_ref[...]).astype(BF16)
    hid = jnp.concatenate(
        [hid[:, e * EXPERT_DIM:(e + 1) * EXPERT_DIM] * g[:, e:e + 1] for e in range(MOE_EC)],
        axis=1)
    sh = _pow2_scale(_row_max_abs(hid).astype(F32))
    hid8 = (hid * sh.astype(BF16)).astype(F8)
    unscale = inv_ref[2] / sh
    for c0 in range(0, wd_ref.shape[1], MOE_NB):
        cols = slice(c0, c0 + MOE_NB)
        acc_ref[:, cols] += _dot(hid8, wd_ref[:, cols]) * unscale

    @pl.when(j == pl.num_programs(1) - 1)
    def _():
        o_ref[...] = acc_ref[...].astype(o_ref.dtype)


def _to_f8(w):
    s = _pow2_scale(jnp.max(jnp.abs(w)))
    return (w * s).astype(F8), 1.0 / s


def _moe(xn8, inv_sx, gates, w_gate, w_up, w_down):
    n, d = xn8.shape
    tm = MOE_TM
    ew = MOE_EC * EXPERT_DIM
    wg, inv_g = _to_f8(w_gate)
    wu, inv_u = _to_f8(w_up)
    wd, inv_d = _to_f8(w_down)
    inv = jnp.stack([inv_g, inv_u, inv_d]).astype(F32)
    row = pl.BlockSpec((tm, d), lambda i, j: (i, 0))
    return pl.pallas_call(
        _moe_kernel,
        grid=(n // tm, N_EXPERTS // MOE_EC),
        in_specs=[pl.BlockSpec(memory_space=pltpu.SMEM),
                  row,
                  pl.BlockSpec((tm, 1), lambda i, j: (i, 0)),
                  pl.BlockSpec((1, tm, MOE_EC), lambda i, j: (j, i, 0)),
                  pl.BlockSpec((d, ew), lambda i, j: (0, j)),
                  pl.BlockSpec((d, ew), lambda i, j: (0, j)),
                  pl.BlockSpec((ew, d), lambda i, j: (j, 0))],
        out_specs=row,
        out_shape=jax.ShapeDtypeStruct((n, d), BF16),
        scratch_shapes=[pltpu.VMEM((tm, d), F32)],
        compiler_params=_cparams("parallel", "arbitrary"),
        name="moe",
    )(inv, xn8, inv_sx, gates, wg, wu, wd)


def _ple_kernel(h_ref, r_ref, p_ref, gple_ref, wpg_ref, wpp_ref, gfin_ref, o_ref, *, final):
    h = h_ref[...] + r_ref[...].astype(F32)
    gate = _sigmoid(_dot(_rms(h, gple_ref[...]).astype(BF16), wpg_ref[...]))
    h = h + gate * _dot(p_ref[...].astype(BF16), wpp_ref[...])
    o_ref[...] = _rms(h, gfin_ref[...]) if final else h


def _ple(h, routed, p2, gple, wpg, wpp, gfin, final):
    n, d = h.shape
    tm = PLE_TM
    row = lambda w: pl.BlockSpec((tm, w), lambda i: (i, 0))
    consts = (gple, wpg, wpp, gfin)
    return pl.pallas_call(
        functools.partial(_ple_kernel, final=final),
        grid=(n // tm,),
        in_specs=[row(d), row(d), row(p2.shape[1])] + [_const_spec(c.shape) for c in consts],
        out_specs=row(d),
        out_shape=jax.ShapeDtypeStruct((n, d), F32),
        compiler_params=_cparams("parallel"),
        name="ple",
    )(h, routed, p2, *consts)


def _rope_tables(seq_len):
    rows = seq_len // GRID_W
    row_ids = jnp.repeat(jnp.arange(rows, dtype=F32), GRID_W)
    col_ids = jnp.tile(jnp.arange(GRID_W, dtype=F32), rows)
    inv_freq = ROPE_THETA ** (-jnp.arange(0, ROT_HALF, 2, dtype=F32) / ROT_HALF)
    ra = row_ids[:, None] * inv_freq[None, :]
    ca = col_ids[:, None] * inv_freq[None, :]
    cos = jnp.concatenate([jnp.cos(ra), jnp.cos(ra), jnp.cos(ca), jnp.cos(ca)], axis=1)
    sin = jnp.concatenate([-jnp.sin(ra), jnp.sin(ra), -jnp.sin(ca), jnp.sin(ca)], axis=1)
    reps = LANES // HEAD_DIM
    return jnp.tile(cos, (1, reps)), jnp.tile(sin, (1, reps))


def _block_diag(t, eye):
    n_dir, g, a, b = t.shape
    return jnp.einsum('dgab,gh->dgahb', t, eye).reshape(n_dir, g * a, g * b)


def _layer(h2, p2, seq_len, final, prm, cos_t, sin_t):
    (g_mix, w_in, q_norm, k_norm, w_attn_up, a_re, a_im, log_dt, b_re, b_im, c_re, c_im,
     d_skip, w_glu, b_glu, w_ssm_up, w_out, g_ffn, w_router, router_bias, w_exp_gate,
     w_exp_up, w_exp_down, w_sh_gate, w_sh_up, w_sh_down, g_ple, w_ple_gate, w_ple_proj,
     g_final) = prm
    n, d = h2.shape
    bsz = n // seq_len
    row = lambda t: t.reshape(1, -1).astype(F32)

    rep = N_HEADS // N_KV_HEADS
    order = [g * rep + j for j in range(rep) for g in range(N_KV_HEADS)]
    qcols = jnp.concatenate([jnp.arange(HEAD_DIM) + hd * HEAD_DIM for hd in order])
    n_main = Q_W + 2 * KV_W + SSM_WIDTH
    w_cat = jnp.concatenate([w_in[:, :Q_W][:, qcols], w_in[:, Q_W:n_main]], axis=1).astype(BF16)
    w_gates, inv_wgates = _to_f8(w_in[:, n_main:].astype(F32))
    reps = LANES // HEAD_DIM
    qg = jnp.tile(row(q_norm), (1, reps))
    kg = jnp.tile(row(k_norm), (1, reps))
    li = jnp.arange(LANES)
    ones_bd = (li[:, None] // HEAD_DIM == li[None, :] // HEAD_DIM).astype(BF16)

    rot = math.sqrt(2.0 * HEAD_DIM)
    w_v = w_in[:, Q_W + KV_W:Q_W + 2 * KV_W].astype(F32)
    v_bound = math.sqrt(d) * jnp.max(jnp.sqrt(jnp.sum((g_mix.astype(F32)[:, None] * w_v) ** 2, axis=0)))
    sq = _pow2_scale(rot * Q_SCALE * jnp.max(jnp.abs(q_norm)).astype(F32))
    sk = _pow2_scale(rot * jnp.max(jnp.abs(k_norm)).astype(F32))
    sv = _pow2_scale(v_bound)
    w_u = w_in[:, Q_W + 2 * KV_W:n_main].astype(F32)
    su = _pow2_scale(math.sqrt(d) * jnp.max(jnp.sqrt(jnp.sum((g_mix.astype(F32)[:, None] * w_u) ** 2,
                                                             axis=0))))
    q0, q1, k, v, u, u8, ga, gs = _proj(jnp.stack([sq, sk, sv, inv_wgates, su]), h2, row(g_mix),
                                        w_cat, w_gates, qg, kg, cos_t, sin_t, ones_bd, seq_len)

    seq = lambda t: t.reshape(bsz, seq_len, t.shape[-1])
    attn = _attention(jnp.stack([1.0 / (sq * sk), 1.0 / sv]), seq(q0), seq(q1), seq(k),
                      seq(v)).reshape(n, Q_W)

    abar_re, abar_im, bbar_re, bbar_im = _ssm_prep(a_re, a_im, log_dt, b_re, b_im)
    eye = jnp.eye(SSM_GROUPS, dtype=F32)
    tr = lambda t: jnp.swapaxes(t, 2, 3)
    bmat, inv_sb = _to_f8(jnp.concatenate([_block_diag(tr(bbar_re), eye),
                                           _block_diag(tr(bbar_im), eye)], axis=2))
    cmat = jnp.concatenate([_block_diag(tr(c_re.astype(F32)), eye),
                            _block_diag(tr(-c_im.astype(F32)), eye)], axis=1)
    cmat = (cmat * (inv_sb / su)).astype(BF16)
    a_vec = jnp.stack([abar_re[0], abar_im[0], abar_re[1], abar_im[1]]).reshape(4, 1, SSM_LANES)
    a_vec = jnp.broadcast_to(a_vec, (4, SSM_B, SSM_LANES))
    yf, yb = _ssm_scan(u8.reshape(bsz, seq_len, SSM_WIDTH), bmat, cmat, a_vec)

    wr_t = w_router.T.astype(F32)
    wrh, wrl = _split_bf16(wr_t)
    h1, xn8, inv_sx, gates = _merge(
        attn, yf.reshape(n, SSM_WIDTH), yb.reshape(n, SSM_WIDTH), u, ga, gs, h2,
        row(d_skip), w_glu.astype(BF16), row(b_glu), w_attn_up[qcols, :].astype(BF16),
        w_ssm_up.astype(BF16), w_out.astype(BF16), row(g_ffn), wrh, wrl,
        router_bias.reshape(-1, 1).astype(F32),
        w_sh_gate.astype(BF16), w_sh_up.astype(BF16), w_sh_down.astype(BF16))

    cat = lambda w: jnp.swapaxes(w, 0, 1).reshape(d, N_EXPERTS * EXPERT_DIM).astype(F32)
    gates_by_step = jnp.swapaxes(gates.reshape(n, N_EXPERTS // MOE_EC, MOE_EC), 0, 1)
    routed = _moe(xn8, inv_sx, gates_by_step, cat(w_exp_gate), cat(w_exp_up),
                  w_exp_down.reshape(N_EXPERTS * EXPERT_DIM, d).astype(F32))

    return _ple(h1, routed, p2, row(g_ple), w_ple_gate.astype(BF16), w_ple_proj.astype(BF16),
                row(g_final), final)


def kernel(x, p, g_mix, w_in, q_norm, k_norm, w_attn_up, a_re, a_im, log_dt, b_re, b_im, c_re,
           c_im, d_skip, w_glu, b_glu, w_ssm_up, w_out, g_ffn, w_router, router_bias,
           w_exp_gate, w_exp_up, w_exp_down, w_sh_gate, w_sh_up, w_sh_down, g_ple,
           w_ple_gate, w_ple_proj, g_final):
    bsz, seq_len, d = x.shape
    depth = p.shape[0]
    stacked = (g_mix, w_in, q_norm, k_norm, w_attn_up, a_re, a_im, log_dt, b_re, b_im, c_re,
               c_im, d_skip, w_glu, b_glu, w_ssm_up, w_out, g_ffn, w_router, router_bias,
               w_exp_gate, w_exp_up, w_exp_down, w_sh_gate, w_sh_up, w_sh_down, g_ple,
               w_ple_gate, w_ple_proj)
    cos_t, sin_t = _rope_tables(seq_len)
    h = x.reshape(bsz * seq_len, d)
    for i in range(depth):
        prm = tuple(t[i] for t in stacked) + (g_final,)
        h = _layer(h, p[i].reshape(bsz * seq_len, -1), seq_len, i == depth - 1, prm,
                   cos_t, sin_t)
    return h.reshape(bsz, seq_len, d)
```

```python
import functools
import math

import jax
import jax.numpy as jnp
from jax import lax
from jax.experimental import pallas as pl
from jax.experimental.pallas import tpu as pltpu

F32 = jnp.float32
BF16 = jnp.bfloat16
F8 = jnp.float8_e4m3fn
F8_TARGET = 256.0
F8_TINY = 1e-30

N_HEADS = 8
N_KV_HEADS = 2
HEAD_DIM = 64
ROPE_THETA = 10000.0
GRID_W = 64
ROT_HALF = HEAD_DIM // 2
ROT_QUARTER = ROT_HALF // 2
SSM_WIDTH = 256
SSM_GROUP = 16
SSM_GROUPS = SSM_WIDTH // SSM_GROUP
SSM_STATE = 64
SSM_LANES = SSM_GROUPS * SSM_STATE
N_EXPERTS = 64
EXPERT_DIM = 128
TOP_K = 8
N_EXPERT_GROUPS = 8
TOPK_GROUPS = 4
EXPERTS_PER_GROUP = N_EXPERTS // N_EXPERT_GROUPS
ROUTED_SCALE = 2.5
EPS = 1e-6
Q_W = N_HEADS * HEAD_DIM
KV_W = N_KV_HEADS * HEAD_DIM
Q_SCALE = math.log2(math.e) / math.sqrt(HEAD_DIM)

LANES = 128
SUBLANES = 8
VMEM_LIMIT = 56 * 1024 * 1024

PROJ_TM = 512
ATTN_TQ = 256
ATTN_KC = 256
SSM_T = 256
SSM_B = SUBLANES
SSM_S = SSM_T + SUBLANES
MERGE_TM = 1024
MOE_TM = 1024
MOE_EC = 16
MOE_NB = 256
PLE_TM = 1024


def _cparams(*sem):
    return pltpu.CompilerParams(dimension_semantics=sem, vmem_limit_bytes=VMEM_LIMIT)


def _const_spec(shape):
    nd = len(shape)
    return pl.BlockSpec(shape, lambda *_: (0,) * nd)


def _dot(a, b):
    return jnp.dot(a, b, preferred_element_type=F32)


def _dot_nt(a, b):
    return lax.dot_general(a, b, (((1,), (1,)), ((), ())), preferred_element_type=F32)


def _split_bf16(x):
    hi = x.astype(BF16)
    lo = (x - hi.astype(F32)).astype(BF16)
    return hi, lo


def _rms(x, g):
    ms = jnp.mean(x * x, axis=-1, keepdims=True)
    return x * lax.rsqrt(ms + EPS) * g


def _sigmoid(x):
    return 1.0 / (1.0 + jnp.exp(-x))


def _silu(x):
    return x * _sigmoid(x)


def _gelu_tanh(x):
    c = math.sqrt(2.0 / math.pi)
    return 0.5 * x * (1.0 + jnp.tanh(c * (x + 0.044715 * (x * x * x))))


def _ssm_prep_kernel(are, aim, ldt, bre, bim, oar, oai, obr, obi):
    dt = jnp.exp(ldt[...])
    lr = are[...]
    li = aim[...]
    mag = jnp.exp(lr * dt)
    ar = mag * jnp.cos(li * dt)
    ai = mag * jnp.sin(li * dt)
    xr = ar - 1.0
    den = lr * lr + li * li
    qr = (xr * lr + ai * li) / den
    qi = (ai * lr - xr * li) / den
    oar[...] = ar
    oai[...] = ai
    obr[...] = qr * bre[...] - qi * bim[...]
    obi[...] = qr * bim[...] + qi * bre[...]


def _ssm_prep(a_re, a_im, log_dt, b_re, b_im):
    n_dir = a_re.shape[0]
    rows = n_dir * SSM_GROUPS * SSM_STATE
    full = (n_dir, SSM_GROUPS, SSM_STATE, SSM_GROUP)
    bc = lambda t: jnp.broadcast_to(t, full).reshape(rows, SSM_GROUP).astype(F32)
    args = (bc(a_re[..., None]), bc(a_im[..., None]), bc(log_dt[..., None, None]),
            b_re.reshape(rows, SSM_GROUP).astype(F32), b_im.reshape(rows, SSM_GROUP).astype(F32))
    shp = jax.ShapeDtypeStruct((rows, SSM_GROUP), F32)
    oar, oai, obr, obi = pl.pallas_call(
        _ssm_prep_kernel, out_shape=(shp, shp, shp, shp), name="ssm_prep")(*args)
    rs = lambda t: t.reshape(full)
    return rs(oar)[..., 0], rs(oai)[..., 0], rs(obr), rs(obi)


def _proj_kernel(sc_ref, x_ref, g_ref, w_ref, wgate_ref, qg_ref, kg_ref, cos_ref, sin_ref, ones_ref,
                 q0_ref, q1_ref, k_ref, v_ref, u_ref, u8_ref, ga_ref, gs_ref):
    tm = x_ref.shape[0]
    xn_f = _rms(x_ref[...], g_ref[...])
    proj = _dot(xn_f.astype(BF16), w_ref[...])
    sx = _pow2_scale(_row_max_abs(xn_f))
    gates_pre = _dot((xn_f * sx).astype(F8), wgate_ref[...]) * (sc_ref[3] / sx)
    cos = cos_ref[...]
    sin = sin_ref[...]
    ones = ones_ref[...]
    lane = lax.broadcasted_iota(jnp.int32, (tm, LANES), 1)
    first = (lane % ROT_HALF) < ROT_QUARTER

    def norm_rope(t, gain):
        hi, lo = _split_bf16(t * t)
        ss = _dot(hi, ones) + _dot(lo, ones)
        tn = t * lax.rsqrt(ss * (1.0 / HEAD_DIM) + EPS) * gain
        partner = jnp.where(first, pltpu.roll(tn, LANES - ROT_QUARTER, 1),
                            pltpu.roll(tn, ROT_QUARTER, 1))
        return tn * cos + partner * sin

    scale = Q_SCALE * sc_ref[0]
    group0 = lane < HEAD_DIM
    for j in range(Q_W // LANES):
        sl = slice(j * LANES, (j + 1) * LANES)
        qj = norm_rope(proj[:, sl], qg_ref[...]) * scale
        q0_ref[:, sl] = jnp.where(group0, qj, 0.0).astype(F8)
        q1_ref[:, sl] = jnp.where(group0, 0.0, qj).astype(F8)
    o = Q_W
    k_ref[...] = (norm_rope(proj[:, o:o + KV_W], kg_ref[...]) * sc_ref[1]).astype(F8)
    o += KV_W
    v_ref[:, :KV_W] = (proj[:, o:o + KV_W] * sc_ref[2]).astype(F8)
    v_ref[:, KV_W:] = jnp.ones((tm, KV_W), F32).astype(F8)
    o += KV_W
    u_ref[...] = proj[:, o:o + SSM_WIDTH].astype(BF16)
    u8_ref[...] = (proj[:, o:o + SSM_WIDTH] * sc_ref[4]).astype(F8)
    d = ga_ref.shape[1]
    ga_ref[...] = _sigmoid(gates_pre[:, :d]).astype(BF16)
    gs_ref[...] = _sigmoid(gates_pre[:, d:]).astype(BF16)


def _proj(scales, x2, g_mix, w_cat, w_gates, qg, kg, cos_t, sin_t, ones_bd, seq_len):
    n, d = x2.shape
    tm = PROJ_TM
    per_seq = seq_len // tm
    row = lambda w: pl.BlockSpec((tm, w), lambda i: (i, 0))
    pos = pl.BlockSpec((tm, LANES), lambda i: (i % per_seq, 0))
    out = lambda w, dt=BF16: jax.ShapeDtypeStruct((n, w), dt)
    return pl.pallas_call(
        _proj_kernel,
        grid=(n // tm,),
        in_specs=[pl.BlockSpec(memory_space=pltpu.SMEM),
                  row(d), _const_spec(g_mix.shape), _const_spec(w_cat.shape),
                  _const_spec(w_gates.shape),
                  _const_spec(qg.shape), _const_spec(kg.shape), pos, pos,
                  _const_spec(ones_bd.shape)],
        out_specs=[row(Q_W), row(Q_W), row(KV_W), row(2 * KV_W), row(SSM_WIDTH), row(SSM_WIDTH),
                   row(d), row(d)],
        out_shape=[out(Q_W, F8), out(Q_W, F8), out(KV_W, F8), out(2 * KV_W, F8),
                   out(SSM_WIDTH), out(SSM_WIDTH, F8), out(d), out(d)],
        compiler_params=_cparams("parallel"),
        name="proj",
    )(scales, x2, g_mix, w_cat, w_gates, qg, kg, cos_t, sin_t, ones_bd)


P_SHIFT = 8.0


def _attn_kernel(inv_ref, q0_ref, q1_ref, k_ref, v_ref, o_ref):
    tq = q0_ref.shape[1]
    k = k_ref[0]
    v = v_ref[0]
    n_col = Q_W // LANES
    group0 = lax.broadcasted_iota(jnp.int32, (tq, LANES), 1) < HEAD_DIM

    def scores(q_ref):
        qs = jnp.concatenate([q_ref[0, :, j * LANES:(j + 1) * LANES] for j in range(n_col)], axis=0)
        return _dot_nt(qs, k)

    def values(s):
        sb = s.astype(BF16)
        x = sb - jnp.max(sb, axis=1, keepdims=True)
        p = jnp.exp2(x * inv_ref[0].astype(BF16) + P_SHIFT).astype(F8)
        r = _dot(p, v)
        return r[:, :KV_W] * (inv_ref[1] / r[:, KV_W:KV_W + 1])

    s = [scores(q_ref) for q_ref in (q0_ref, q1_ref)]
    o = [values(sg) for sg in s]
    for j in range(n_col):
        rows = slice(j * tq, (j + 1) * tq)
        o_ref[0, :, j * LANES:(j + 1) * LANES] = jnp.where(group0, o[0][rows], o[1][rows]).astype(BF16)


def _attention(inv, q0, q1, k, v):
    b, l, _ = q0.shape
    tq = ATTN_TQ
    qspec = pl.BlockSpec((1, tq, Q_W), lambda i, j: (i, j, 0))
    return pl.pallas_call(
        _attn_kernel,
        grid=(b, l // tq),
        in_specs=[pl.BlockSpec(memory_space=pltpu.SMEM), qspec, qspec,
                  pl.BlockSpec((1, l, KV_W), lambda i, j: (i, 0, 0)),
                  pl.BlockSpec((1, l, 2 * KV_W), lambda i, j: (i, 0, 0))],
        out_specs=qspec,
        out_shape=jax.ShapeDtypeStruct((b, l, Q_W), BF16),
        compiler_params=_cparams("parallel", "parallel"),
        name="attention",
    )(inv, q0, q1, k, v)


def _ssm_kernel(uf_ref, ub_ref, bmat_ref, cmat_ref, a_ref, yf_ref, yb_ref, buf, carry):
    t_len = uf_ref.shape[1]
    c = pl.program_id(1)

    @pl.when(c == 0)
    def _():
        carry[...] = jnp.zeros_like(carry)

    nt = SSM_LANES // LANES
    for s in range(SSM_B):
        rows = slice(s * SSM_S, s * SSM_S + t_len)
        for d, u_ref in enumerate((uf_ref, ub_ref)):
            bu = _dot(u_ref[s], bmat_ref[d])
            for c in range(2 * nt):
                buf[d * 2 * nt + c, rows, :] = bu[:, c * LANES:(c + 1) * LANES]

    for d in range(2):
        coef = [(a_ref[2 * d, :, c * LANES:(c + 1) * LANES],
                 a_ref[2 * d + 1, :, c * LANES:(c + 1) * LANES]) for c in range(nt)]

        def step(i, st, d=d, coef=coef):
            t = i if d == 0 else t_len - 1 - i
            rows = pl.ds(t, SSM_B, stride=SSM_S)
            new = []
            for c in range(nt):
                pr, pi = d * 2 * nt + c, d * 2 * nt + nt + c
                ar, ai = coef[c]
                sr, si = st[2 * c], st[2 * c + 1]
                nr = ar * sr - ai * si + buf[pr, rows, :]
                ni = ar * si + ai * sr + buf[pi, rows, :]
                buf[pr, rows, :] = nr
                buf[pi, rows, :] = ni
                new += [nr, ni]
            return tuple(new)

        base = d * 2 * nt
        st = lax.fori_loop(0, t_len, step, tuple(carry[base + n] for n in range(2 * nt)))
        for n in range(2 * nt):
            carry[base + n] = st[n]

    for s in range(SSM_B):
        rows = slice(s * SSM_S, s * SSM_S + t_len)
        for d, y_ref in enumerate((yf_ref, yb_ref)):
            xs = jnp.concatenate([buf[d * 2 * nt + c, rows, :] for c in range(2 * nt)],
                                 axis=1).astype(BF16)
            y_ref[s] = _dot(xs, cmat_ref[d]).astype(y_ref.dtype)


def _ssm_scan(u, bmat, cmat, a_vec):
    b, l, w = u.shape
    t = SSM_T
    nc = l // t
    blk = (SSM_B, t, w)
    planes = 4 * SSM_LANES // LANES
    out = jax.ShapeDtypeStruct((b, l, w), BF16)
    return pl.pallas_call(
        _ssm_kernel,
        grid=(b // SSM_B, nc),
        in_specs=[pl.BlockSpec(blk, lambda i, c: (i, c, 0)),
                  pl.BlockSpec(blk, lambda i, c: (i, nc - 1 - c, 0)),
                  _const_spec(bmat.shape), _const_spec(cmat.shape), _const_spec(a_vec.shape)],
        out_specs=[pl.BlockSpec(blk, lambda i, c: (i, c, 0)),
                   pl.BlockSpec(blk, lambda i, c: (i, nc - 1 - c, 0))],
        out_shape=[out, out],
        scratch_shapes=[pltpu.VMEM((planes, SSM_B * SSM_S, LANES), F32),
                        pltpu.VMEM((planes, SSM_B, LANES), F32)],
        compiler_params=_cparams("parallel", "arbitrary"),
        name="ssm_scan",
    )(u, u, bmat, cmat, a_vec)


def _first_index_of_max(vals, idx, n, axis):
    m = jnp.max(vals, axis=axis, keepdims=True)
    return jnp.min(jnp.where(vals == m, idx, n), axis=axis, keepdims=True)


def _route(scores, bias):
    tm = scores.shape[1]
    neg = -jnp.inf
    biased = scores + bias
    b3 = biased.reshape(N_EXPERT_GROUPS, EXPERTS_PER_GROUP, tm)
    j3 = lax.broadcasted_iota(jnp.int32, b3.shape, 1)
    m1 = jnp.max(b3, axis=1, keepdims=True)
    f1 = jnp.min(jnp.where(b3 == m1, j3, EXPERTS_PER_GROUP), axis=1, keepdims=True)
    m2 = jnp.max(jnp.where(j3 == f1, neg, b3), axis=1, keepdims=True)
    gscore = (m1 + m2)[:, 0, :]
    gi = lax.broadcasted_iota(jnp.int32, gscore.shape, 0)
    gsel = jnp.zeros(gscore.shape, F32)
    cur = gscore
    for _ in range(TOPK_GROUPS):
        pick = gi == _first_index_of_max(cur, gi, N_EXPERT_GROUPS, 0)
        gsel = jnp.where(pick, 1.0, gsel)
        cur = jnp.where(pick, neg, cur)
    emask = jnp.broadcast_to(gsel[:, None, :], b3.shape) > 0.0
    masked = jnp.where(emask, b3, -1e30).reshape(N_EXPERTS, tm)
    ei = lax.broadcasted_iota(jnp.int32, masked.shape, 0)
    top_w = jnp.zeros(masked.shape, F32)
    cur = masked
    for _ in range(TOP_K):
        pick = ei == _first_index_of_max(cur, ei, N_EXPERTS, 0)
        top_w = jnp.where(pick, scores, top_w)
        cur = jnp.where(pick, neg, cur)
    return top_w / jnp.sum(top_w, axis=0, keepdims=True) * ROUTED_SCALE


def _merge_kernel(attn_ref, yf_ref, yb_ref, u_ref, ga_ref, gs_ref, x_ref,
                  dskip_ref, wglu_ref, bglu_ref, wau_ref, wsu_ref, wout_ref, gffn_ref,
                  wrh_ref, wrl_ref, rbias_ref, wsg_ref, wsup_ref, wsd_ref,
                  h_ref, xn8_ref, isx_ref, gates_ref):
    y = (dskip_ref[...] * u_ref[...].astype(F32) + yf_ref[...].astype(F32)
         + yb_ref[...].astype(F32))
    y = _gelu_tanh(y)
    ssm = y * _sigmoid(_dot(y.astype(BF16), wglu_ref[...]) + bglu_ref[...])
    merged = (ga_ref[...] * _dot(attn_ref[...], wau_ref[...]).astype(BF16)
              + gs_ref[...] * _dot(ssm.astype(BF16), wsu_ref[...]).astype(BF16))
    h = x_ref[...] + _dot(merged, wout_ref[...])
    xn = _rms(h, gffn_ref[...])
    xh, xl = _split_bf16(xn)
    sx = _pow2_scale(_row_max_abs(xn))
    xn8_ref[...] = (xn * sx).astype(F8)
    isx_ref[...] = 1.0 / sx
    sh = _silu(_dot(xh, wsg_ref[...])) * _dot(xh, wsup_ref[...])
    h_ref[...] = h + _dot(sh.astype(BF16), wsd_ref[...])
    wh = wrh_ref[...]
    logits = _dot_nt(wh, xh) + _dot_nt(wh, xl) + _dot_nt(wrl_ref[...], xh)
    gates_ref[...] = _route(_sigmoid(logits), rbias_ref[...]).T


def _merge(attn, yf, yb, u, ga, gs, x2, dskip, wglu, bglu, wau, wsu, wout, gffn, wrh, wrl, rbias,
           wsg, wsup, wsd):
    n, d = x2.shape
    tm = MERGE_TM
    row = lambda w: pl.BlockSpec((tm, w), lambda i: (i, 0))
    consts = (dskip, wglu, bglu, wau, wsu, wout, gffn, wrh, wrl, rbias, wsg, wsup, wsd)
    return pl.pallas_call(
        _merge_kernel,
        grid=(n // tm,),
        in_specs=[row(Q_W), row(SSM_WIDTH), row(SSM_WIDTH), row(SSM_WIDTH), row(d), row(d), row(d)]
                 + [_const_spec(c.shape) for c in consts],
        out_specs=[row(d), row(d), row(1), row(N_EXPERTS)],
        out_shape=[jax.ShapeDtypeStruct((n, d), F32), jax.ShapeDtypeStruct((n, d), F8),
                   jax.ShapeDtypeStruct((n, 1), F32), jax.ShapeDtypeStruct((n, N_EXPERTS), F32)],
        compiler_params=_cparams("parallel"),
        name="merge",
    )(attn, yf, yb, u, ga, gs, x2, *consts)


def _pow2_scale(max_abs):
    return jnp.exp2(jnp.floor(jnp.log2(F8_TARGET / jnp.maximum(max_abs, F8_TINY))))


def _row_max_abs(t):
    return jnp.max(jnp.abs(t), axis=1, keepdims=True)


def _moe_kernel(inv_ref, x8_ref, isx_ref, gates_ref, wg_ref, wu_ref, wd_ref, o_ref, acc_ref,
                hid_ref):
    j = pl.program_id(1)

    @pl.when(j == 0)
    def _():
        acc_ref[...] = jnp.zeros_like(acc_ref)

    x8 = x8_ref[...]
    inv_sx = isx_ref[...]
    g_unscale = (inv_sx * inv_ref[0]).astype(BF16)
    g = (gates_ref[0] * (inv_sx * inv_ref[1])).astype(BF16)
    row_max = None
    for c in range(MOE_EC // 2):
        cols = slice(c * 2 * EXPERT_DIM, (c + 1) * 2 * EXPERT_DIM)
        hc = (_silu(_dot(x8, wg_ref[:, cols]).astype(BF16) * g_unscale)
              * _dot(x8, wu_ref[:, cols]).astype(BF16))
        hc = jnp.concatenate(
            [hc[:, e * EXPERT_DIM:(e + 1) * EXPERT_DIM] * g[:, 2 * c + e:2 * c + e + 1]
             for e in range(2)], axis=1)
        hid_ref[:, cols] = hc
        mc = _row_max_abs(hc)
        row_max = mc if row_max is None else jnp.maximum(row_max, mc)
    sh = _pow2_scale(row_max.astype(F32))
    hid8 = (hid_ref[...] * sh.astype(BF16)).astype(F8)
    unscale = inv_ref[2] / sh
    for c0 in range(0, wd_ref.shape[1], MOE_NB):
        cols = slice(c0, c0 + MOE_NB)
        acc_ref[:, cols] += _dot(hid8, wd_ref[:, cols]) * unscale

    @pl.when(j == pl.num_programs(1) - 1)
    def _():
        o_ref[...] = acc_ref[...].astype(o_ref.dtype)


def _to_f8(w):
    s = _pow2_scale(jnp.max(jnp.abs(w)))
    return (w * s).astype(F8), 1.0 / s


def _moe(xn8, inv_sx, gates, w_gate, w_up, w_down):
    n, d = xn8.shape
    tm = MOE_TM
    ew = MOE_EC * EXPERT_DIM
    wg, inv_g = _to_f8(w_gate)
    wu, inv_u = _to_f8(w_up)
    wd, inv_d = _to_f8(w_down)
    inv = jnp.stack([inv_g, inv_u, inv_d]).astype(F32)
    row = pl.BlockSpec((tm, d), lambda i, j: (i, 0))
    return pl.pallas_call(
        _moe_kernel,
        grid=(n // tm, N_EXPERTS // MOE_EC),
        in_specs=[pl.BlockSpec(memory_space=pltpu.SMEM),
                  row,
                  pl.BlockSpec((tm, 1), lambda i, j: (i, 0)),
                  pl.BlockSpec((1, tm, MOE_EC), lambda i, j: (j, i, 0)),
                  pl.BlockSpec((d, ew), lambda i, j: (0, j)),
                  pl.BlockSpec((d, ew), lambda i, j: (0, j)),
                  pl.BlockSpec((ew, d), lambda i, j: (j, 0))],
        out_specs=row,
        out_shape=jax.ShapeDtypeStruct((n, d), BF16),
        scratch_shapes=[pltpu.VMEM((tm, d), F32), pltpu.VMEM((tm, ew), BF16)],
        compiler_params=_cparams("parallel", "arbitrary"),
        name="moe",
    )(inv, xn8, inv_sx, gates, wg, wu, wd)


def _ple_kernel(h_ref, r_ref, p_ref, gple_ref, wpg_ref, wpp_ref, gfin_ref, o_ref, *, final):
    h = h_ref[...] + r_ref[...].astype(F32)
    gate = _sigmoid(_dot(_rms(h, gple_ref[...]).astype(BF16), wpg_ref[...]))
    h = h + gate * _dot(p_ref[...].astype(BF16), wpp_ref[...])
    o_ref[...] = _rms(h, gfin_ref[...]) if final else h


def _ple(h, routed, p2, gple, wpg, wpp, gfin, final):
    n, d = h.shape
    tm = PLE_TM
    row = lambda w: pl.BlockSpec((tm, w), lambda i: (i, 0))
    consts = (gple, wpg, wpp, gfin)
    return pl.pallas_call(
        functools.partial(_ple_kernel, final=final),
        grid=(n // tm,),
        in_specs=[row(d), row(d), row(p2.shape[1])] + [_const_spec(c.shape) for c in consts],
        out_specs=row(d),
        out_shape=jax.ShapeDtypeStruct((n, d), F32),
        compiler_params=_cparams("parallel"),
        name="ple",
    )(h, routed, p2, *consts)


def _rope_tables(seq_len):
    rows = seq_len // GRID_W
    row_ids = jnp.repeat(jnp.arange(rows, dtype=F32), GRID_W)
    col_ids = jnp.tile(jnp.arange(GRID_W, dtype=F32), rows)
    inv_freq = ROPE_THETA ** (-jnp.arange(0, ROT_HALF, 2, dtype=F32) / ROT_HALF)
    ra = row_ids[:, None] * inv_freq[None, :]
    ca = col_ids[:, None] * inv_freq[None, :]
    cos = jnp.concatenate([jnp.cos(ra), jnp.cos(ra), jnp.cos(ca), jnp.cos(ca)], axis=1)
    sin = jnp.concatenate([-jnp.sin(ra), jnp.sin(ra), -jnp.sin(ca), jnp.sin(ca)], axis=1)
    reps = LANES // HEAD_DIM
    return jnp.tile(cos, (1, reps)), jnp.tile(sin, (1, reps))


def _block_diag(t, eye):
    n_dir, g, a, b = t.shape
    return jnp.einsum('dgab,gh->dgahb', t, eye).reshape(n_dir, g * a, g * b)


def _layer(h2, p2, seq_len, final, prm, cos_t, sin_t):
    (g_mix, w_in, q_norm, k_norm, w_attn_up, a_re, a_im, log_dt, b_re, b_im, c_re, c_im,
     d_skip, w_glu, b_glu, w_ssm_up, w_out, g_ffn, w_router, router_bias, w_exp_gate,
     w_exp_up, w_exp_down, w_sh_gate, w_sh_up, w_sh_down, g_ple, w_ple_gate, w_ple_proj,
     g_final) = prm
    n, d = h2.shape
    bsz = n // seq_len
    row = lambda t: t.reshape(1, -1).astype(F32)

    rep = N_HEADS // N_KV_HEADS
    order = [g * rep + j for j in range(rep) for g in range(N_KV_HEADS)]
    qcols = jnp.concatenate([jnp.arange(HEAD_DIM) + hd * HEAD_DIM for hd in order])
    n_main = Q_W + 2 * KV_W + SSM_WIDTH
    w_cat = jnp.concatenate([w_in[:, :Q_W][:, qcols], w_in[:, Q_W:n_main]], axis=1).astype(BF16)
    w_gates, inv_wgates = _to_f8(w_in[:, n_main:].astype(F32))
    reps = LANES // HEAD_DIM
    qg = jnp.tile(row(q_norm), (1, reps))
    kg = jnp.tile(row(k_norm), (1, reps))
    li = jnp.arange(LANES)
    ones_bd = (li[:, None] // HEAD_DIM == li[None, :] // HEAD_DIM).astype(BF16)

    rot = math.sqrt(2.0 * HEAD_DIM)
    w_v = w_in[:, Q_W + KV_W:Q_W + 2 * KV_W].astype(F32)
    v_bound = math.sqrt(d) * jnp.max(jnp.sqrt(jnp.sum((g_mix.astype(F32)[:, None] * w_v) ** 2, axis=0)))
    sq = _pow2_scale(rot * Q_SCALE * jnp.max(jnp.abs(q_norm)).astype(F32))
    sk = _pow2_scale(rot * jnp.max(jnp.abs(k_norm)).astype(F32))
    sv = _pow2_scale(v_bound)
    w_u = w_in[:, Q_W + 2 * KV_W:n_main].astype(F32)
    su = _pow2_scale(math.sqrt(d) * jnp.max(jnp.sqrt(jnp.sum((g_mix.astype(F32)[:, None] * w_u) ** 2,
                                                             axis=0))))
    q0, q1, k, v, u, u8, ga, gs = _proj(jnp.stack([sq, sk, sv, inv_wgates, su]), h2, row(g_mix),
                                        w_cat, w_gates, qg, kg, cos_t, sin_t, ones_bd, seq_len)

    seq = lambda t: t.reshape(bsz, seq_len, t.shape[-1])
    attn = _attention(jnp.stack([1.0 / (sq * sk), 1.0 / sv]), seq(q0), seq(q1), seq(k),
                      seq(v)).reshape(n, Q_W)

    abar_re, abar_im, bbar_re, bbar_im = _ssm_prep(a_re, a_im, log_dt, b_re, b_im)
    eye = jnp.eye(SSM_GROUPS, dtype=F32)
    tr = lambda t: jnp.swapaxes(t, 2, 3)
    bmat, inv_sb = _to_f8(jnp.concatenate([_block_diag(tr(bbar_re), eye),
                                           _block_diag(tr(bbar_im), eye)], axis=2))
    cmat = jnp.concatenate([_block_diag(tr(c_re.astype(F32)), eye),
                            _block_diag(tr(-c_im.astype(F32)), eye)], axis=1)
    cmat = (cmat * (inv_sb / su)).astype(BF16)
    a_vec = jnp.stack([abar_re[0], abar_im[0], abar_re[1], abar_im[1]]).reshape(4, 1, SSM_LANES)
    a_vec = jnp.broadcast_to(a_vec, (4, SSM_B, SSM_LANES))
    yf, yb = _ssm_scan(u8.reshape(bsz, seq_len, SSM_WIDTH), bmat, cmat, a_vec)

    wr_t = w_router.T.astype(F32)
    wrh, wrl = _split_bf16(wr_t)
    h1, xn8, inv_sx, gates = _merge(
        attn, yf.reshape(n, SSM_WIDTH), yb.reshape(n, SSM_WIDTH), u, ga, gs, h2,
        row(d_skip), w_glu.astype(BF16), row(b_glu), w_attn_up[qcols, :].astype(BF16),
        w_ssm_up.astype(BF16), w_out.astype(BF16), row(g_ffn), wrh, wrl,
        router_bias.reshape(-1, 1).astype(F32),
        w_sh_gate.astype(BF16), w_sh_up.astype(BF16), w_sh_down.astype(BF16))

    cat = lambda w: jnp.swapaxes(w, 0, 1).reshape(d, N_EXPERTS * EXPERT_DIM).astype(F32)
    gates_by_step = jnp.swapaxes(gates.reshape(n, N_EXPERTS // MOE_EC, MOE_EC), 0, 1)
    routed = _moe(xn8, inv_sx, gates_by_step, cat(w_exp_gate), cat(w_exp_up),
                  w_exp_down.reshape(N_EXPERTS * EXPERT_DIM, d).astype(F32))

    return _ple(h1, routed, p2, row(g_ple), w_ple_gate.astype(BF16), w_ple_proj.astype(BF16),
                row(g_final), final)


def kernel(x, p, g_mix, w_in, q_norm, k_norm, w_attn_up, a_re, a_im, log_dt, b_re, b_im, c_re,
           c_im, d_skip, w_glu, b_glu, w_ssm_up, w_out, g_ffn, w_router, router_bias,
           w_exp_gate, w_exp_up, w_exp_down, w_sh_gate, w_sh_up, w_sh_down, g_ple,
           w_ple_gate, w_ple_proj, g_final):
    bsz, seq_len, d = x.shape
    depth = p.shape[0]
    stacked = (g_mix, w_in, q_norm, k_norm, w_attn_up, a_re, a_im, log_dt, b_re, b_im, c_re,
               c_im, d_skip, w_glu, b_glu, w_ssm_up, w_out, g_ffn, w_router, router_bias,
               w_exp_gate, w_exp_up, w_exp_down, w_sh_gate, w_sh_up, w_sh_down, g_ple,
               w_ple_gate, w_ple_proj)
    cos_t, sin_t = _rope_tables(seq_len)
    h = x.reshape(bsz * seq_len, d)
    for i in range(depth):
        prm = tuple(t[i] for t in stacked) + (g_final,)
        h = _layer(h, p[i].reshape(bsz * seq_len, -1), seq_len, i == depth - 1, prm,
                   cos_t, sin_t)
    return h.reshape(bsz, seq_len, d)
```
